```python
import math
import jax, jax.numpy as jnp
from jax import lax
import numpy as np

D_MODEL = 2048
BATCH = 2
SEQ = 4096
DEPTH = 4
DEC_BATCH = 8
DEC_SEQ = 4
PAST_LEN = 16384
PAGE_SIZE = 128

RET_HEADS = 8
RET_DK = D_MODEL // 16
RET_DV = D_MODEL // 16
RET_CHUNK = 128
DIFF_HEADS = 8
DIFF_DH = D_MODEL // 32
DIFF_DV = 2 * DIFF_DH
Q_BLOCK = 128
ROPE_THETA = 10000.0
CONV_CH = D_MODEL // 2
CONV_WIDTH = 31
D_FF = 5632
FFN_CONV_WIDTH = 3
N_BRANCH = 3
EPS = 1e-6
W_IN_COLS = (2 * RET_HEADS * RET_DK + 2 * RET_HEADS * RET_DV + 4 * DIFF_HEADS * DIFF_DH
             + DIFF_HEADS * DIFF_DV + 2 * CONV_CH + N_BRANCH * D_MODEL)
BRANCH_ROWS = RET_HEADS * RET_DV + DIFF_HEADS * DIFF_DV + CONV_CH

kernel_name = "hybrid_retention_diffattn_conformer_step"


def _in_split_points():
    sizes = (RET_HEADS * RET_DK, RET_HEADS * RET_DK, RET_HEADS * RET_DV, RET_HEADS * RET_DV,
             2 * DIFF_HEADS * DIFF_DH, 2 * DIFF_HEADS * DIFF_DH, DIFF_HEADS * DIFF_DV,
             2 * CONV_CH, N_BRANCH * D_MODEL)
    pts, acc = [], 0
    for s in sizes[:-1]:
        acc += s
        pts.append(acc)
    return pts


def rmsnorm(x, g):
    xf = x.astype(jnp.float32)
    y = xf * lax.rsqrt(jnp.mean(xf * xf, axis=-1, keepdims=True) + EPS)
    return (y * g.astype(jnp.float32)).astype(x.dtype)


def head_rms(x):
    xf = x.astype(jnp.float32)
    return (xf * lax.rsqrt(jnp.mean(xf * xf, axis=-1, keepdims=True) + EPS)).astype(x.dtype)


def layernorm(x, g, b):
    xf = x.astype(jnp.float32)
    mu = jnp.mean(xf, axis=-1, keepdims=True)
    var = jnp.mean(jnp.square(xf - mu), axis=-1, keepdims=True)
    y = (xf - mu) * lax.rsqrt(var + EPS) * g.astype(jnp.float32) + b.astype(jnp.float32)
    return y.astype(x.dtype)


def rope(x, pos):
    d = x.shape[-1]
    half = d // 2
    inv = ROPE_THETA ** (-jnp.arange(half, dtype=jnp.float32) * 2.0 / d)
    ang = pos.astype(jnp.float32)[:, None] * inv[None, :]
    cos = jnp.cos(ang)[:, None, :]
    sin = jnp.sin(ang)[:, None, :]
    xf = x.astype(jnp.float32)
    x1, x2 = xf[..., :half], xf[..., half:]
    return jnp.concatenate([x1 * cos - x2 * sin, x2 * cos + x1 * sin], axis=-1).astype(x.dtype)


def causal_dwconv(x, prefix, w, b):
    C = x.shape[-1]
    W = w.shape[0]
    xp = jnp.concatenate([prefix.astype(x.dtype), x], axis=1)
    y = lax.conv_general_dilated(xp, w[:, None, :].astype(x.dtype), window_strides=(1,),
                                 padding='VALID', dimension_numbers=('NWC', 'WIO', 'NWC'),
                                 feature_group_count=C)
    return y + b.astype(x.dtype), xp[:, xp.shape[1] - (W - 1):]


def retention(q, k, v, S0, chunk):
    B, L, H, dk = q.shape
    dt = q.dtype
    n = L // chunk
    log_g = jnp.log1p(-jnp.exp2(-5.0 - jnp.arange(H, dtype=jnp.float32)))
    idx = jnp.arange(chunk, dtype=jnp.float32)
    dist = idx[:, None] - idx[None, :]
    D = jnp.where(dist >= 0, jnp.exp(log_g[:, None, None] * jnp.maximum(dist, 0.0)), 0.0).astype(dt)
    xi = jnp.exp(log_g[:, None] * (idx + 1.0)).astype(dt)
    zeta = jnp.exp(log_g[:, None] * (chunk - 1.0 - idx)).astype(dt)
    g_chunk = jnp.exp(log_g * chunk).astype(dt)
    k = k * (dk ** -0.5)

    def to_chunks(t):
        return t.reshape(B, n, chunk, H, t.shape[-1]).transpose(1, 0, 3, 2, 4)

    def step(S, inp):
        qc, kc, vc = inp
        inner = jnp.einsum('bhnd,bhmd->bhnm', qc, kc) * D[None]
        o = (jnp.einsum('bhnm,bhme->bhne', inner, vc)
             + jnp.einsum('bhnd,bhde->bhne', qc, S) * xi[None, :, :, None])
        S = S * g_chunk[None, :, None, None] + jnp.einsum('bhmd,bhme->bhde', kc * zeta[None, :, :, None], vc)
        return S, o

    S, o = lax.scan(step, S0.astype(dt), (to_chunks(q), to_chunks(k), to_chunks(v)))
    o = o.transpose(1, 0, 3, 2, 4).reshape(B, L, H, v.shape[-1])
    return o, S


def diff_attend(q, k, v, q_pos, k_pos, lam):
    s = jnp.einsum('bhqd,bhkd->bhqk', q, k).astype(jnp.float32) * (q.shape[-1] ** -0.5)
    s = jnp.where(k_pos[None, :] <= q_pos[:, None], s, -1e30)
    p = jax.nn.softmax(s, axis=-1)
    B, H2, Lq, Lk = p.shape
    p = p.reshape(B, H2 // 2, 2, Lq, Lk)
    a = p[:, :, 0] - lam * p[:, :, 1]
    return jnp.einsum('bhqk,bhkd->bhqd', a.astype(v.dtype), v)


def mixer_block(x, pos, ret_state, conv_buf, k_past, v_past, ret_chunk, lam_init,
                norm_g, w_in, lq1, lk1, lq2, lk2, subln_g, conv_w, conv_b, ln_g, ln_b, w_branch, w_o):
    B, L, _ = x.shape
    xn = rmsnorm(x, norm_g)
    h = xn @ w_in
    rq, rk, rv, rg, dq, dk, dv, cu, gt = jnp.split(h, _in_split_points(), axis=-1)

    rq = rope(rq.reshape(B, L, RET_HEADS, RET_DK), pos)
    rk = rope(rk.reshape(B, L, RET_HEADS, RET_DK), pos)
    ro, ret_new = retention(rq, rk, rv.reshape(B, L, RET_HEADS, RET_DV), ret_state, ret_chunk)
    ro = (head_rms(ro) * jax.nn.silu(rg.reshape(B, L, RET_HEADS, RET_DV))).reshape(B, L, RET_HEADS * RET_DV)

    dq = rope(dq.reshape(B, L, 2 * DIFF_HEADS, DIFF_DH), pos)
    dk = rope(dk.reshape(B, L, 2 * DIFF_HEADS, DIFF_DH), pos)
    dv = dv.reshape(B, L, DIFF_HEADS, DIFF_DV)
    lam = (jnp.exp(jnp.sum(lq1.astype(jnp.float32) * lk1.astype(jnp.float32)))
           - jnp.exp(jnp.sum(lq2.astype(jnp.float32) * lk2.astype(jnp.float32))) + lam_init)
    qT = dq.transpose(0, 2, 1, 3)
    if k_past is None:
        kT = dk.transpose(0, 2, 1, 3)
        vT = dv.transpose(0, 2, 1, 3)
        nb = L // Q_BLOCK
        qb = qT.reshape(B, 2 * DIFF_HEADS, nb, Q_BLOCK, DIFF_DH).transpose(2, 0, 1, 3, 4)
        pb = pos.reshape(nb, Q_BLOCK)
        ob = lax.map(lambda a: diff_attend(a[0], kT, vT, a[1], pos, lam), (qb, pb))
        do = ob.transpose(1, 0, 3, 2, 4).reshape(B, L, DIFF_HEADS, DIFF_DV)
    else:
        k_all = jnp.concatenate([k_past.astype(dk.dtype), dk], axis=1).transpose(0, 2, 1, 3)
        v_all = jnp.concatenate([v_past.astype(dv.dtype), dv], axis=1).transpose(0, 2, 1, 3)
        k_pos = jnp.arange(k_all.shape[2], dtype=jnp.int32)
        do = diff_attend(qT, k_all, v_all, pos, k_pos, lam).transpose(0, 2, 1, 3)
    do = (head_rms(do) * subln_g.astype(do.dtype) * (1.0 - lam_init)).reshape(B, L, DIFF_HEADS * DIFF_DV)

    ca, cb = jnp.split(cu, 2, axis=-1)
    glu = ca * jax.nn.sigmoid(cb)
    co, conv_new = causal_dwconv(glu, conv_buf, conv_w, conv_b)
    co = jax.nn.silu(layernorm(co, ln_g, ln_b))

    gates = jax.nn.sigmoid(gt).reshape(B, L, N_BRANCH, D_MODEL)
    r1 = RET_HEADS * RET_DV
    r2 = r1 + DIFF_HEADS * DIFF_DV
    m = (gates[:, :, 0] * (ro @ w_branch[:r1]) + gates[:, :, 1] * (do @ w_branch[r1:r2])
         + gates[:, :, 2] * (co @ w_branch[r2:]))
    return x + m @ w_o, (ret_new, conv_new, dk, dv)


def ffn_block(x, ffn_buf, norm_g, w_up, dw_w, dw_b, w_down):
    xn = rmsnorm(x, norm_g)
    u = xn @ w_up
    u, buf_new = causal_dwconv(u, ffn_buf, dw_w, dw_b)
    ug, uv = jnp.split(u, 2, axis=-1)
    return x + (jax.nn.silu(ug) * uv) @ w_down, buf_new


def setup_inputs(seed: int = 0) -> dict:
    key = jax.random.key(seed)
    ks = jax.random.split(key, 32)
    n_pages = PAST_LEN // PAGE_SIZE
    n_pool = (5 * DEC_BATCH * n_pages + 3) // 4
    f32 = jnp.float32

    def nrm(k, shape, scale):
        return jax.random.normal(k, shape, f32) * scale

    page_table = jax.random.permutation(ks[0], n_pool)[:DEC_BATCH * n_pages].reshape(DEC_BATCH, n_pages).astype(jnp.int32)
    return {
        "x_prompt": nrm(ks[1], (BATCH, SEQ, D_MODEL), 1.0),
        "x_sample": nrm(ks[2], (DEC_BATCH, DEC_SEQ, D_MODEL), 1.0),
        "cache_k": nrm(ks[3], (DEPTH, n_pool, PAGE_SIZE, 2 * DIFF_HEADS, DIFF_DH), 1.0),
        "cache_v": nrm(ks[4], (DEPTH, n_pool, PAGE_SIZE, DIFF_HEADS, DIFF_DV), 1.0),
        "page_table": page_table,
        "state_ret": nrm(ks[5], (DEPTH, DEC_BATCH, RET_HEADS, RET_DK, RET_DV), 1.0),
        "state_conv": nrm(ks[6], (DEPTH, DEC_BATCH, CONV_WIDTH - 1, CONV_CH), 0.5),
        "state_ffn": nrm(ks[7], (DEPTH, DEC_BATCH, FFN_CONV_WIDTH - 1, 2 * D_FF), 1.0),
        "norm_mix": 1.0 + nrm(ks[8], (DEPTH, D_MODEL), 0.02),
        "w_in": nrm(ks[9], (DEPTH, D_MODEL, W_IN_COLS), D_MODEL ** -0.5),
        "lambda_q1": nrm(ks[10], (DEPTH, DIFF_DH), 0.1),
        "lambda_k1": nrm(ks[11], (DEPTH, DIFF_DH), 0.1),
        "lambda_q2": nrm(ks[12], (DEPTH, DIFF_DH), 0.1),
        "lambda_k2": nrm(ks[13], (DEPTH, DIFF_DH), 0.1),
        "diff_subln": 1.0 + nrm(ks[14], (DEPTH, DIFF_DV), 0.02),
        "conv_w": nrm(ks[15], (DEPTH, CONV_WIDTH, CONV_CH), CONV_WIDTH ** -0.5),
        "conv_b": nrm(ks[16], (DEPTH, CONV_CH), 0.02),
        "conv_ln_g": 1.0 + nrm(ks[17], (DEPTH, CONV_CH), 0.02),
        "conv_ln_b": nrm(ks[18], (DEPTH, CONV_CH), 0.02),
        "w_branch": nrm(ks[19], (DEPTH, BRANCH_ROWS, D_MODEL), (RET_HEADS * RET_DV) ** -0.5),
        "w_o": nrm(ks[20], (DEPTH, D_MODEL, D_MODEL), D_MODEL ** -0.5),
        "norm_ffn": 1.0 + nrm(ks[21], (DEPTH, D_MODEL), 0.02),
        "ffn_up": nrm(ks[22], (DEPTH, D_MODEL, 2 * D_FF), D_MODEL ** -0.5),
        "ffn_dw_w": nrm(ks[23], (DEPTH, FFN_CONV_WIDTH, 2 * D_FF), FFN_CONV_WIDTH ** -0.5),
        "ffn_dw_b": nrm(ks[24], (DEPTH, 2 * D_FF), 0.02),
        "ffn_down": nrm(ks[25], (DEPTH, D_FF, D_MODEL), D_FF ** -0.5),
        "norm_final": 1.0 + nrm(ks[26], (D_MODEL,), 0.02),
    }


def reference(x_prompt, x_sample, cache_k, cache_v, page_table, state_ret, state_conv, state_ffn,
              norm_mix, w_in, lambda_q1, lambda_k1, lambda_q2, lambda_k2, diff_subln,
              conv_w, conv_b, conv_ln_g, conv_ln_b, w_branch, w_o,
              norm_ffn, ffn_up, ffn_dw_w, ffn_dw_b, ffn_down, norm_final):
    Bp, Lp, _ = x_prompt.shape
    Bs, Ls, _ = x_sample.shape
    past_len = page_table.shape[1] * PAGE_SIZE
    dt = x_prompt.dtype
    pos_p = jnp.arange(Lp, dtype=jnp.int32)
    pos_s = past_len + jnp.arange(Ls, dtype=jnp.int32)
    zero_ret = jnp.zeros((Bp, RET_HEADS, RET_DK, RET_DV), dt)
    zero_conv = jnp.zeros((Bp, CONV_WIDTH - 1, CONV_CH), dt)
    zero_ffn = jnp.zeros((Bp, FFN_CONV_WIDTH - 1, 2 * D_FF), dt)

    xp, xs = x_prompt, x_sample
    kp_l, vp_l, rp_l, cp_l, fp_l = [], [], [], [], []
    ks_l, vs_l, rs_l, cs_l, fs_l = [], [], [], [], []
    for l in range(DEPTH):
        lam_init = 0.8 - 0.6 * math.exp(-0.3 * l)
        mix_w = (norm_mix[l], w_in[l], lambda_q1[l], lambda_k1[l], lambda_q2[l], lambda_k2[l],
                 diff_subln[l], conv_w[l], conv_b[l], conv_ln_g[l], conv_ln_b[l], w_branch[l], w_o[l])
        ffn_w = (norm_ffn[l], ffn_up[l], ffn_dw_w[l], ffn_dw_b[l], ffn_down[l])

        xp, (rp, cp, kp, vp) = mixer_block(xp, pos_p, zero_ret, zero_conv, None, None,
                                           RET_CHUNK, lam_init, *mix_w)
        xp, fp = ffn_block(xp, zero_ffn, *ffn_w)

        k_past = cache_k[l, page_table].reshape(Bs, past_len, 2 * DIFF_HEADS, DIFF_DH)
        v_past = cache_v[l, page_table].reshape(Bs, past_len, DIFF_HEADS, DIFF_DV)
        xs, (rs, cs, ks_, vs_) = mixer_block(xs, pos_s, state_ret[l], state_conv[l], k_past, v_past,
                                             Ls, lam_init, *mix_w)
        xs, fs = ffn_block(xs, state_ffn[l], *ffn_w)

        kp_l.append(kp); vp_l.append(vp); rp_l.append(rp); cp_l.append(cp); fp_l.append(fp)
        ks_l.append(ks_); vs_l.append(vs_); rs_l.append(rs); cs_l.append(cs); fs_l.append(fs)

    y_prompt = rmsnorm(xp, norm_final)
    y_sample = rmsnorm(xs, norm_final)
    return (y_prompt, y_sample,
            jnp.stack(kp_l), jnp.stack(vp_l), jnp.stack(rp_l), jnp.stack(cp_l), jnp.stack(fp_l),
            jnp.stack(ks_l), jnp.stack(vs_l), jnp.stack(rs_l), jnp.stack(cs_l), jnp.stack(fs_l))
```

```python
import functools
import math

import jax
import jax.numpy as jnp
from jax import lax
from jax.experimental import pallas as pl
from jax.experimental.pallas import tpu as pltpu

F32 = jnp.float32
BF16 = jnp.bfloat16

D_MODEL = 2048
DEPTH = 4
PAGE_SIZE = 128
RET_HEADS = 8
RET_D = 128
DIFF_HEADS = 8
DIFF_DH = 64
DIFF_DV = 128
ROPE_THETA = 10000.0
CONV_CH = 1024
CONV_WIDTH = 31
D_FF = 5632
FFN_CONV_WIDTH = 3
EPS = 1e-6
LANES = 128
SUBLANES = 8
COL = 1024
CB_RQ, CB_RK, CB_RV, CB_RG, CB_DQ, CB_DK, CB_DV, CB_CA, CB_CB, CB_GATE = 0, 1, 2, 3, 4, 5, 6, 7, 8, 9
W_IN_COLS = 15 * COL
NEG = -1e30
VMEM_LIMIT = 56 * 1024 * 1024
SAMPLE_ROWS = 8


def _params(n_grid):
    return pltpu.CompilerParams(dimension_semantics=("arbitrary",) * n_grid,
                                vmem_limit_bytes=VMEM_LIMIT)


def _call(body, *, grid, in_specs, out_specs, out_shape, scratch=(), prefetch, args, name):
    spec = pltpu.PrefetchScalarGridSpec(num_scalar_prefetch=len(prefetch), grid=grid,
                                        in_specs=in_specs, out_specs=out_specs,
                                        scratch_shapes=list(scratch))
    return pl.pallas_call(body, grid_spec=spec, out_shape=out_shape,
                          compiler_params=_params(len(grid)), name=name)(*prefetch, *args)


def _sigmoid(x):
    return 1.0 / (1.0 + jnp.exp(-x))


def _silu(x):
    return x * _sigmoid(x)


def _rmsnorm_body(l_ref, x_ref, g_ref, o_ref):
    x = x_ref[...]
    inv = lax.rsqrt(jnp.mean(x * x, axis=-1, keepdims=True) + EPS)
    o_ref[...] = (x * inv * g_ref[...]).astype(o_ref.dtype)


def rmsnorm(lidx, x, g, out_dtype):
    m, d = x.shape
    tm = min(m, 512)
    return _call(
        _rmsnorm_body, grid=(m // tm,),
        in_specs=[pl.BlockSpec((tm, d), lambda i, l: (i, 0)),
                  pl.BlockSpec((None, 1, d), lambda i, l: (l[0], 0, 0))],
        out_specs=pl.BlockSpec((tm, d), lambda i, l: (i, 0)),
        out_shape=jax.ShapeDtypeStruct((m, d), out_dtype),
        prefetch=(lidx,), args=(x, g), name="rmsnorm")


def _rope_cols(x, cos, sin, head_dim):
    if head_dim == LANES:
        rot = pltpu.roll(x, LANES // 2, 1)
    else:
        half = head_dim // 2
        lane = lax.broadcasted_iota(jnp.int32, x.shape, 1)
        rot = jnp.where((lane % head_dim) < half, pltpu.roll(x, LANES - half, 1), pltpu.roll(x, half, 1))
    return x * cos + rot * sin


def _mm_body(l_ref, x_ref, w_ref, *rest, mode):
    acc = jnp.dot(x_ref[...], w_ref[...], preferred_element_type=F32)
    if mode == "plain":
        (o_ref,) = rest
        o_ref[...] = acc.astype(o_ref.dtype)
    elif mode == "res":
        r_ref, o_ref = rest
        o_ref[...] = r_ref[...] + acc
    else:
        c128, s128, c64, s64, o_ref = rest
        j = pl.program_id(0)
        is128 = jnp.logical_or(j == CB_RQ, j == CB_RK)
        is64 = jnp.logical_or(j == CB_DQ, j == CB_DK)

        @pl.when(is128)
        def _():
            for g in range(COL // LANES):
                cs = slice(g * LANES, (g + 1) * LANES)
                o_ref[:, cs] = _rope_cols(acc[:, cs], c128[...], s128[...], RET_D)

        @pl.when(is64)
        def _():
            for g in range(COL // LANES):
                cs = slice(g * LANES, (g + 1) * LANES)
                o_ref[:, cs] = _rope_cols(acc[:, cs], c64[...], s64[...], DIFF_DH)

        @pl.when(jnp.logical_not(jnp.logical_or(is128, is64)))
        def _():
            o_ref[...] = acc


def matmul(lidx, x, w, *, tn, mode="plain", res=None, rope=None, out_dtype=F32):
    m, k = x.shape
    n = w.shape[-1]
    tm = min(m, 512)
    grid = (n // tn, m // tm)
    in_specs = [pl.BlockSpec((tm, k), lambda j, i, l: (i, 0)),
                pl.BlockSpec((None, k, tn), lambda j, i, l: (l[0], 0, j))]
    args = [x, w]
    if mode == "res":
        in_specs.append(pl.BlockSpec((tm, tn), lambda j, i, l: (i, j)))
        args.append(res)
    elif mode == "rope":
        assert tn == COL
        nt = rope[0].shape[0] // tm
        for t in rope:
            in_specs.append(pl.BlockSpec((tm, LANES), lambda j, i, l: (i % nt, 0)))
            args.append(t)
    return _call(
        functools.partial(_mm_body, mode=mode), grid=grid, in_specs=in_specs,
        out_specs=pl.BlockSpec((tm, tn), lambda j, i, l: (i, j)),
        out_shape=jax.ShapeDtypeStruct((m, n), out_dtype),
        prefetch=(lidx,), args=args, name="mm_" + mode)


def _ret_body(l_ref, q_ref, k_ref, v_ref, g_ref, s0_ref, d_ref, xi_ref, zeta_ref, gc_ref,
              o_ref, sout_ref, s_scr, *, chunk):
    t = pl.program_id(1)

    @pl.when(t == 0)
    def _():
        s_scr[...] = s0_ref[...]

    rows = q_ref.shape[0]
    cp = d_ref.shape[-1]
    nt_dims = (((1,), (1,)), ((), ()))
    tn_dims = (((0,), (0,)), ((), ()))
    for h in range(RET_HEADS):
        cs = slice(h * RET_D, (h + 1) * RET_D)
        for c in range(rows // chunk):
            rs = slice(c * chunk, (c + 1) * chunk)
            q = q_ref[rs, cs]
            k = k_ref[rs, cs] * (RET_D ** -0.5)
            v = v_ref[rs, cs]
            if cp > chunk:
                pad = jnp.zeros((cp - chunk, RET_D), F32)
                q, k, v = (jnp.concatenate([a, pad], axis=0) for a in (q, k, v))
            qb, kb, vb = q.astype(BF16), k.astype(BF16), v.astype(BF16)
            inner = lax.dot_general(qb, kb, nt_dims, preferred_element_type=F32) * d_ref[h]
            s = s_scr[h]
            o = (jnp.dot(inner.astype(BF16), vb, preferred_element_type=F32)
                 + jnp.dot(qb, s.astype(BF16), preferred_element_type=F32) * xi_ref[h])
            kz = (k * zeta_ref[h]).astype(BF16)
            s_scr[h] = s * gc_ref[h] + lax.dot_general(kz, vb, tn_dims, preferred_element_type=F32)
            o = o[:chunk]
            o = o * lax.rsqrt(jnp.mean(o * o, axis=-1, keepdims=True) + EPS)
            o_ref[rs, cs] = (o * _silu(g_ref[rs, cs])).astype(o_ref.dtype)

    @pl.when(t == pl.num_programs(1) - 1)
    def _():
        sout_ref[...] = s_scr[...]


def _ret_consts(cp, nvalid):
    log_g = jnp.log1p(-jnp.exp2(-5.0 - jnp.arange(RET_HEADS, dtype=F32)))
    idx = jnp.arange(cp, dtype=F32)
    dist = idx[:, None] - idx[None, :]
    d = jnp.where(dist >= 0, jnp.exp(log_g[:, None, None] * jnp.maximum(dist, 0.0)), 0.0)
    xi = jnp.exp(log_g[:, None] * (idx + 1.0))[:, :, None]
    zeta = jnp.where(idx < nvalid, jnp.exp(log_g[:, None] * (nvalid - 1.0 - idx)), 0.0)[:, :, None]
    gc = jnp.exp(log_g * nvalid)[:, None, None]
    return d.astype(F32), xi.astype(F32), zeta.astype(F32), gc.astype(F32)


def retention(lidx, h3, s0, *, rows, chunk, nvalid, s0_layer):
    b, seq, _ = h3.shape
    cp = max(chunk, LANES)
    d, xi, zeta, gc = _ret_consts(cp, nvalid)

    def hspec(cb):
        return pl.BlockSpec((None, rows, COL), lambda bi, t, l: (bi, t, cb))

    if s0_layer:
        s0_spec = pl.BlockSpec((None, None, RET_HEADS, RET_D, RET_D), lambda bi, t, l: (l[0], bi, 0, 0, 0))
    else:
        s0_spec = pl.BlockSpec((None, RET_HEADS, RET_D, RET_D), lambda bi, t, l: (bi, 0, 0, 0))
    full3 = lambda a: pl.BlockSpec(a.shape, lambda bi, t, l: (0, 0, 0))
    return _call(
        functools.partial(_ret_body, chunk=chunk), grid=(b, seq // rows),
        in_specs=[hspec(CB_RQ), hspec(CB_RK), hspec(CB_RV), hspec(CB_RG), s0_spec,
                  full3(d), full3(xi), full3(zeta), full3(gc)],
        out_specs=[pl.BlockSpec((None, rows, COL), lambda bi, t, l: (bi, t, 0)),
                   pl.BlockSpec((None, RET_HEADS, RET_D, RET_D), lambda bi, t, l: (bi, 0, 0, 0))],
        out_shape=[jax.ShapeDtypeStruct((b, seq, COL), BF16),
                   jax.ShapeDtypeStruct((b, RET_HEADS, RET_D, RET_D), F32)],
        scratch=[pltpu.VMEM((RET_HEADS, RET_D, RET_D), F32)],
        prefetch=(lidx,), args=(h3, h3, h3, h3, s0, d, xi, zeta, gc), name="retention")


def _lambda(lq1, lk1, lq2, lk2, lam_init):
    a = jnp.exp(jnp.sum(lq1[...] * lk1[...], axis=-1, keepdims=True))
    b = jnp.exp(jnp.sum(lq2[...] * lk2[...], axis=-1, keepdims=True))
    return a - b + lam_init


def _diff_out(o, sub_ref, lam_init):
    o = o * lax.rsqrt(jnp.mean(o * o, axis=-1, keepdims=True) + EPS)
    return o * sub_ref[...] * (1.0 - lam_init)


def _softmax_step(s, vb, m, l, acc):
    m_new = jnp.maximum(m, jnp.max(s, axis=-1, keepdims=True))
    p = jnp.exp(s - m_new)
    alpha = jnp.exp(m - m_new)
    l = alpha * l + jnp.sum(p, axis=-1, keepdims=True)
    acc = alpha * acc + jnp.dot(p.astype(BF16), vb, preferred_element_type=F32)
    return m_new, l, acc


def _flash_body(l_ref, q_ref, k_ref, v_ref, lq1, lk1, lq2, lk2, sub_ref, li_ref, o_ref, *, tq):
    qi = pl.program_id(2)
    lam_init = li_ref[l_ref[0]]
    q = q_ref[...] * (DIFF_DH ** -0.5)
    lane = lax.broadcasted_iota(jnp.int32, q.shape, 1)
    q2 = jnp.concatenate([jnp.where(lane < DIFF_DH, q, 0.0), jnp.where(lane >= DIFF_DH, q, 0.0)],
                         axis=0).astype(BF16)
    nt_dims = (((1,), (1,)), ((), ()))

    def block(j, carry, masked):
        off = pl.multiple_of(j * tq, tq)
        kb = k_ref[pl.ds(off, tq), :].astype(BF16)
        vb = v_ref[pl.ds(off, tq), :].astype(BF16)
        s = lax.dot_general(q2, kb, nt_dims, preferred_element_type=F32)
        if masked:
            row = lax.broadcasted_iota(jnp.int32, s.shape, 0) % tq
            col = lax.broadcasted_iota(jnp.int32, s.shape, 1)
            s = jnp.where(col <= row, s, NEG)
        return _softmax_step(s, vb, *carry)

    init = (jnp.full((2 * tq, 1), NEG, F32), jnp.zeros((2 * tq, 1), F32), jnp.zeros((2 * tq, DIFF_DV), F32))
    carry = lax.fori_loop(0, qi, lambda j, c: block(j, c, False), init)
    m, l, acc = block(qi, carry, True)
    lam = _lambda(lq1, lk1, lq2, lk2, lam_init)
    o = acc[:tq] / l[:tq] - lam * (acc[tq:] / l[tq:])
    o_ref[...] = _diff_out(o, sub_ref, lam_init).astype(o_ref.dtype)


def _lam_specs(nidx):
    def vec(width):
        if nidx == 3:
            return pl.BlockSpec((None, 1, width), lambda a, b, c, l: (l[0], 0, 0))
        return pl.BlockSpec((None, 1, width), lambda a, b, l, pt: (l[0], 0, 0))
    return [vec(DIFF_DH)] * 4 + [vec(DIFF_DV), pl.BlockSpec(memory_space=pltpu.SMEM)]


def flash_diff_attention(lidx, h3, lam_vecs, subln, lam_inits, *, tq):
    b, seq, _ = h3.shape
    per = COL // LANES

    def kvspec(cb):
        return pl.BlockSpec((None, seq, LANES), lambda bi, p, qi, l: (bi, 0, cb * per + p))

    return _call(
        functools.partial(_flash_body, tq=tq), grid=(b, DIFF_HEADS, seq // tq),
        in_specs=[pl.BlockSpec((None, tq, LANES), lambda bi, p, qi, l: (bi, qi, CB_DQ * per + p)),
                  kvspec(CB_DK), kvspec(CB_DV)] + _lam_specs(3),
        out_specs=pl.BlockSpec((None, tq, LANES), lambda bi, p, qi, l: (bi, qi, p)),
        out_shape=jax.ShapeDtypeStruct((b, seq, COL), BF16),
        prefetch=(lidx,), args=(h3, h3, h3, *lam_vecs, subln, lam_inits), name="flash_diff")


def _decode_body(l_ref, pt_ref, *refs, group):
    k_pages = refs[:group]
    v_pages = refs[group:2 * group]
    (q_ref, kn_ref, vn_ref, lq1, lk1, lq2, lk2, sub_ref, li_ref,
     o_ref, qbd_scr, m_scr, l_scr, acc_scr) = refs[2 * group:]
    s_id = pl.program_id(1)
    rows = 2 * DIFF_HEADS * SAMPLE_ROWS
    nt_dims = (((1,), (1,)), ((), ()))

    @pl.when(s_id == 0)
    def _():
        q = q_ref[...] * (DIFF_DH ** -0.5)
        qt = jnp.concatenate([q] * (2 * DIFF_HEADS), axis=0)
        rh = lax.broadcasted_iota(jnp.int32, qt.shape, 0) // SAMPLE_ROWS
        ch = lax.broadcasted_iota(jnp.int32, qt.shape, 1) // DIFF_DH
        qbd_scr[...] = jnp.where(rh == ch, qt, 0.0).astype(BF16)
        m_scr[...] = jnp.full(m_scr.shape, NEG, F32)
        l_scr[...] = jnp.zeros(l_scr.shape, F32)
        acc_scr[...] = jnp.zeros(acc_scr.shape, F32)

    qbd = qbd_scr[...]
    s = jnp.concatenate(
        [lax.dot_general(qbd, kp[...].astype(BF16), nt_dims, preferred_element_type=F32) for kp in k_pages],
        axis=1)
    m = m_scr[...]
    m_new = jnp.maximum(m, jnp.max(s, axis=-1, keepdims=True))
    p = jnp.exp(s - m_new).astype(BF16)
    alpha = jnp.exp(m - m_new)
    l_scr[...] = alpha * l_scr[...] + jnp.sum(p.astype(F32), axis=-1, keepdims=True)
    pv = jnp.zeros((rows, COL), F32)
    for g, vp in enumerate(v_pages):
        pv = pv + jnp.dot(p[:, g * PAGE_SIZE:(g + 1) * PAGE_SIZE], vp[...].astype(BF16),
                          preferred_element_type=F32)
    acc_scr[...] = alpha * acc_scr[...] + pv
    m_scr[...] = m_new

    @pl.when(s_id == pl.num_programs(1) - 1)
    def _():
        pad = jnp.zeros((LANES - SAMPLE_ROWS, COL), F32)
        kn = jnp.concatenate([kn_ref[...], pad], axis=0).astype(BF16)
        vn = jnp.concatenate([vn_ref[...], pad], axis=0).astype(BF16)
        sn = lax.dot_general(qbd, kn, nt_dims, preferred_element_type=F32)
        tok = lax.broadcasted_iota(jnp.int32, sn.shape, 0) % SAMPLE_ROWS
        key = lax.broadcasted_iota(jnp.int32, sn.shape, 1)
        sn = jnp.where(key <= tok, sn, NEG)
        mf, lf, accf = _softmax_step(sn, vn, m_scr[...], l_scr[...], acc_scr[...])
        lam_init = li_ref[l_ref[0]]
        lam = _lambda(lq1, lk1, lq2, lk2, lam_init)
        on = accf / lf
        for hp in range(DIFF_HEADS):
            r1 = slice((2 * hp) * SAMPLE_ROWS, (2 * hp + 1) * SAMPLE_ROWS)
            r2 = slice((2 * hp + 1) * SAMPLE_ROWS, (2 * hp + 2) * SAMPLE_ROWS)
            cs = slice(hp * DIFF_DV, (hp + 1) * DIFF_DV)
            o = on[r1, cs] - lam * on[r2, cs]
            o_ref[:, cs] = _diff_out(o, sub_ref, lam_init).astype(o_ref.dtype)


def decode_diff_attention(lidx, page_table, cache_k, cache_v, h3, lam_vecs, subln, lam_inits, *, group):
    b = h3.shape[0]
    n_pages = page_table.shape[1]
    rows = 2 * DIFF_HEADS * SAMPLE_ROWS

    def page_spec(g):
        return pl.BlockSpec((None, None, PAGE_SIZE, COL),
                            lambda bi, s, l, pt: (l[0], pt[bi, s * group + g], 0, 0))

    def hspec(cb):
        return pl.BlockSpec((None, SAMPLE_ROWS, COL), lambda bi, s, l, pt: (bi, 0, cb))

    pages = [page_spec(g) for g in range(group)]
    return _call(
        functools.partial(_decode_body, group=group), grid=(b, n_pages // group),
        in_specs=pages + pages + [hspec(CB_DQ), hspec(CB_DK), hspec(CB_DV)] + _lam_specs(2),
        out_specs=pl.BlockSpec((None, SAMPLE_ROWS, COL), lambda bi, s, l, pt: (bi, 0, 0)),
        out_shape=jax.ShapeDtypeStruct((b, SAMPLE_ROWS, COL), BF16),
        scratch=[pltpu.VMEM((rows, COL), BF16), pltpu.VMEM((rows, 1), F32), pltpu.VMEM((rows, 1), F32),
                 pltpu.VMEM((rows, COL), F32)],
        prefetch=(lidx, page_table),
        args=([cache_k] * group + [cache_v] * group + [h3, h3, h3] + list(lam_vecs) + [subln, lam_inits]),
        name="decode_diff")


CONV_HALO = 32
CONV_RC = 32


def _conv_body(l_ref, ca_ref, cb_ref, pre_ref, w_ref, b_ref, lg_ref, lb_ref, o_ref, st_ref, buf, y_scr,
               *, nvalid_last):
    t = pl.program_id(1)
    rows = ca_ref.shape[0]

    @pl.when(t == 0)
    def _():
        buf[0:CONV_HALO] = pre_ref[...]

    if rows >= CONV_HALO:
        @pl.when(t > 0)
        def _():
            buf[0:CONV_HALO] = buf[rows:rows + CONV_HALO]

    buf[CONV_HALO:CONV_HALO + rows] = ca_ref[...] * _sigmoid(cb_ref[...])
    base = CONV_HALO - (CONV_WIDTH - 1)
    rc = min(rows, CONV_RC)
    for r0 in range(0, rows, rc):
        acc = jnp.zeros((rc, CONV_CH), F32) + b_ref[...]
        for j in range(CONV_WIDTH):
            acc = acc + buf[base + r0 + j:base + r0 + j + rc] * w_ref[j:j + 1]
        y_scr[r0:r0 + rc] = acc
    y = y_scr[...]
    mu = jnp.mean(y, axis=-1, keepdims=True)
    yc = y - mu
    var = jnp.mean(yc * yc, axis=-1, keepdims=True)
    y = yc * lax.rsqrt(var + EPS) * lg_ref[...] + lb_ref[...]
    o_ref[...] = _silu(y).astype(o_ref.dtype)

    @pl.when(t == pl.num_programs(1) - 1)
    def _():
        st_ref[...] = buf[nvalid_last:nvalid_last + CONV_HALO]


def conv_module(lidx, h3, prefix, conv_w, conv_b, ln_g, ln_b, *, rows, nvalid_last, prefix_layer):
    b, seq, _ = h3.shape
    if prefix_layer:
        pre_spec = pl.BlockSpec((None, None, CONV_HALO, CONV_CH), lambda bi, t, l: (l[0], bi, 0, 0))
    else:
        pre_spec = pl.BlockSpec((None, CONV_HALO, CONV_CH), lambda bi, t, l: (bi, 0, 0))
    vec = pl.BlockSpec((None, 1, CONV_CH), lambda bi, t, l: (l[0], 0, 0))
    return _call(
        functools.partial(_conv_body, nvalid_last=nvalid_last), grid=(b, seq // rows),
        in_specs=[pl.BlockSpec((None, rows, COL), lambda bi, t, l: (bi, t, CB_CA)),
                  pl.BlockSpec((None, rows, COL), lambda bi, t, l: (bi, t, CB_CB)),
                  pre_spec,
                  pl.BlockSpec((None, CONV_WIDTH, CONV_CH), lambda bi, t, l: (l[0], 0, 0)),
                  vec, vec, vec],
        out_specs=[pl.BlockSpec((None, rows, CONV_CH), lambda bi, t, l: (bi, t, 0)),
                   pl.BlockSpec((None, CONV_HALO, CONV_CH), lambda bi, t, l: (bi, 0, 0))],
        out_shape=[jax.ShapeDtypeStruct((b, seq, CONV_CH), BF16),
                   jax.ShapeDtypeStruct((b, CONV_HALO, CONV_CH), F32)],
        scratch=[pltpu.VMEM((CONV_HALO + rows, CONV_CH), F32), pltpu.VMEM((rows, CONV_CH), F32)],
        prefetch=(lidx,), args=(h3, h3, prefix, conv_w, conv_b, ln_g, ln_b), name="conv_module")


def _merge_body(l_ref, ro_ref, do_ref, co_ref, w0, w1, w2, g0, g1, g2, o_ref):
    acc = _sigmoid(g0[...]) * jnp.dot(ro_ref[...], w0[...], preferred_element_type=F32)
    acc = acc + _sigmoid(g1[...]) * jnp.dot(do_ref[...], w1[...], preferred_element_type=F32)
    acc = acc + _sigmoid(g2[...]) * jnp.dot(co_ref[...], w2[...], preferred_element_type=F32)
    o_ref[...] = acc.astype(o_ref.dtype)


def merge(lidx, ro, do, co, h, w_branch):
    m = ro.shape[0]
    tm = min(m, 512)
    tn = COL
    nb = D_MODEL // tn
    act = pl.BlockSpec((tm, COL), lambda j, i, l: (i, 0))

    def wspec(r):
        return pl.BlockSpec((None, COL, tn), lambda j, i, l: (l[0], r, j))

    def gspec(r):
        return pl.BlockSpec((tm, tn), lambda j, i, l: (i, CB_GATE + r * nb + j))

    return _call(
        _merge_body, grid=(nb, m // tm),
        in_specs=[act, act, act, wspec(0), wspec(1), wspec(2), gspec(0), gspec(1), gspec(2)],
        out_specs=pl.BlockSpec((tm, tn), lambda j, i, l: (i, j)),
        out_shape=jax.ShapeDtypeStruct((m, D_MODEL), BF16),
        prefetch=(lidx,), args=(ro, do, co, w_branch, w_branch, w_branch, h, h, h), name="merge")


FFN_TN = 512
FFN_HALO = 8


def _ffn_gate_body(l_ref, ug_ref, uv_ref, pg_ref, pv_ref, wg_ref, wv_ref, bg_ref, bv_ref,
                   o_ref, sg_ref, sv_ref, buf, *, nvalid_last):
    t = pl.program_id(2)
    rows = ug_ref.shape[0]
    ys = []
    for half, (u_ref, p_ref, w_ref, b_ref, s_ref) in enumerate(
            ((ug_ref, pg_ref, wg_ref, bg_ref, sg_ref), (uv_ref, pv_ref, wv_ref, bv_ref, sv_ref))):
        @pl.when(t == 0)
        def _():
            buf[half, 0:FFN_HALO] = p_ref[...]

        if rows >= FFN_HALO:
            @pl.when(t > 0)
            def _():
                buf[half, 0:FFN_HALO] = buf[half, rows:rows + FFN_HALO]

        buf[half, FFN_HALO:FFN_HALO + rows] = u_ref[...]
        base = FFN_HALO - (FFN_CONV_WIDTH - 1)
        y = b_ref[...] + buf[half, base:base + rows] * w_ref[0:1]
        for j in range(1, FFN_CONV_WIDTH):
            y = y + buf[half, base + j:base + j + rows] * w_ref[j:j + 1]
        ys.append(y)

        @pl.when(t == pl.num_programs(2) - 1)
        def _():
            s_ref[...] = buf[half, nvalid_last:nvalid_last + FFN_HALO]

    o_ref[...] = (_silu(ys[0]) * ys[1]).astype(o_ref.dtype)


def ffn_gate(lidx, u3, prefix, dw_w, dw_b, *, rows, nvalid_last, prefix_layer):
    b, seq, _ = u3.shape
    nc = D_FF // FFN_TN

    def pspec(off):
        if prefix_layer:
            return pl.BlockSpec((None, None, FFN_HALO, FFN_TN), lambda bi, c, t, l: (l[0], bi, 0, c + off))
        return pl.BlockSpec((None, FFN_HALO, FFN_TN), lambda bi, c, t, l: (bi, 0, c + off))

    def uspec(off):
        return pl.BlockSpec((None, rows, FFN_TN), lambda bi, c, t, l: (bi, t, c + off))

    def wspec(off):
        return pl.BlockSpec((None, FFN_CONV_WIDTH, FFN_TN), lambda bi, c, t, l: (l[0], 0, c + off))

    def bspec(off):
        return pl.BlockSpec((None, 1, FFN_TN), lambda bi, c, t, l: (l[0], 0, c + off))

    st_spec = pl.BlockSpec((None, FFN_HALO, FFN_TN), lambda bi, c, t, l: (bi, 0, c))
    st_shape = jax.ShapeDtypeStruct((b, FFN_HALO, D_FF), F32)
    return _call(
        functools.partial(_ffn_gate_body, nvalid_last=nvalid_last), grid=(b, nc, seq // rows),
        in_specs=[uspec(0), uspec(nc), pspec(0), pspec(nc), wspec(0), wspec(nc), bspec(0), bspec(nc)],
        out_specs=[pl.BlockSpec((None, rows, FFN_TN), lambda bi, c, t, l: (bi, t, c)), st_spec, st_spec],
        out_shape=[jax.ShapeDtypeStruct((b, seq, D_FF), BF16), st_shape, st_shape],
        scratch=[pltpu.VMEM((2, FFN_HALO + rows, FFN_TN), F32)],
        prefetch=(lidx,), args=(u3, u3, prefix, prefix, dw_w, dw_w, dw_b, dw_b), name="ffn_gate")


def _rope_tables(pos, head_dim):
    half = head_dim // 2
    inv = ROPE_THETA ** (-jnp.arange(half, dtype=F32) * 2.0 / head_dim)
    ang = pos.astype(F32)[:, None] * inv[None, :]
    cos, sin = jnp.cos(ang), jnp.sin(ang)
    reps = LANES // head_dim
    return (jnp.tile(jnp.concatenate([cos, cos], axis=-1), (1, reps)),
            jnp.tile(jnp.concatenate([-sin, sin], axis=-1), (1, reps)))


def _layer(lidx, x, w, rope, lam_inits, *, batch, seq, attend, ret_state, conv_prefix, ffn_prefix,
           rows, ret_chunk, nvalid, state_layer):
    m = batch * seq
    xn = rmsnorm(lidx, x, w["norm_mix"], BF16)
    h = matmul(lidx, xn, w["w_in"], tn=COL, mode="rope", rope=rope)
    h3 = h.reshape(batch, seq, W_IN_COLS)
    ro, ret_new = retention(lidx, h3, ret_state, rows=rows, chunk=ret_chunk, nvalid=nvalid, s0_layer=state_layer)
    do = attend(h3)
    co, conv_new = conv_module(lidx, h3, conv_prefix, w["conv_w"], w["conv_b"], w["conv_ln_g"], w["conv_ln_b"],
                               rows=rows, nvalid_last=nvalid if seq == rows else rows,
                               prefix_layer=state_layer)
    mg = merge(lidx, ro.reshape(m, COL), do.reshape(m, COL), co.reshape(m, COL), h, w["w_branch"])
    x = matmul(lidx, mg, w["w_o"], tn=COL, mode="res", res=x)
    xn = rmsnorm(lidx, x, w["norm_ffn"], BF16)
    u = matmul(lidx, xn, w["ffn_up"], tn=COL)
    a, st_g, st_v = ffn_gate(lidx, u.reshape(batch, seq, 2 * D_FF), ffn_prefix, w["ffn_dw_w"], w["ffn_dw_b"],
                             rows=rows, nvalid_last=nvalid if seq == rows else rows, prefix_layer=state_layer)
    x = matmul(lidx, a.reshape(m, D_FF), w["ffn_down"], tn=FFN_TN, mode="res", res=x)
    k_rows = h3[:, :, CB_DK * COL:(CB_DK + 1) * COL]
    v_rows = h3[:, :, CB_DV * COL:(CB_DV + 1) * COL]
    ffn_new = jnp.concatenate([st_g, st_v], axis=-1)
    return x, (k_rows, v_rows, ret_new, conv_new, ffn_new)


def kernel(x_prompt, x_sample, cache_k, cache_v, page_table, state_ret, state_conv, state_ffn, norm_mix, w_in, lambda_q1, lambda_k1, lambda_q2, lambda_k2, diff_subln, conv_w, conv_b, conv_ln_g, conv_ln_b, w_branch, w_o, norm_ffn, ffn_up, ffn_dw_w, ffn_dw_b, ffn_down, norm_final):
    bp, lp, _ = x_prompt.shape
    bs, ls, _ = x_sample.shape
    depth = w_in.shape[0]
    n_pool = cache_k.shape[1]
    past_len = page_table.shape[1] * PAGE_SIZE
    sr = SAMPLE_ROWS

    vec3 = lambda a: a.reshape(a.shape[0], 1, a.shape[-1])
    w = dict(norm_mix=vec3(norm_mix), w_in=w_in.astype(BF16), conv_w=conv_w, conv_b=vec3(conv_b),
             conv_ln_g=vec3(conv_ln_g), conv_ln_b=vec3(conv_ln_b), w_branch=w_branch.astype(BF16),
             w_o=w_o.astype(BF16), norm_ffn=vec3(norm_ffn), ffn_up=ffn_up.astype(BF16),
             ffn_dw_w=ffn_dw_w, ffn_dw_b=vec3(ffn_dw_b), ffn_down=ffn_down.astype(BF16))
    lam_vecs = tuple(vec3(a) for a in (lambda_q1, lambda_k1, lambda_q2, lambda_k2))
    subln = vec3(diff_subln)
    lam_inits = jnp.asarray([0.8 - 0.6 * math.exp(-0.3 * l) for l in range(depth)], F32)

    pos_p = jnp.arange(lp, dtype=jnp.int32)
    pos_s = past_len + jnp.arange(sr, dtype=jnp.int32)
    rope_p = _rope_tables(pos_p, RET_D) + _rope_tables(pos_p, DIFF_DH)
    rope_s = tuple(jnp.tile(t, (bs, 1)) for t in _rope_tables(pos_s, RET_D) + _rope_tables(pos_s, DIFF_DH))

    ck = cache_k.reshape(depth, n_pool, PAGE_SIZE, COL)
    cv = cache_v.reshape(depth, n_pool, PAGE_SIZE, COL)
    conv_pre_s = jnp.pad(state_conv, ((0, 0), (0, 0), (CONV_HALO - (CONV_WIDTH - 1), 0), (0, 0)))
    ffn_pre_s = jnp.pad(state_ffn, ((0, 0), (0, 0), (FFN_HALO - (FFN_CONV_WIDTH - 1), 0), (0, 0)))
    zero_ret = jnp.zeros((bp, RET_HEADS, RET_D, RET_D), F32)
    zero_conv = jnp.zeros((bp, CONV_HALO, CONV_CH), F32)
    zero_ffn = jnp.zeros((bp, FFN_HALO, 2 * D_FF), F32)

    xp = x_prompt.reshape(bp * lp, D_MODEL)
    xs = jnp.pad(x_sample, ((0, 0), (0, sr - ls), (0, 0))).reshape(bs * sr, D_MODEL)

    outs_p, outs_s = [], []
    for layer in range(depth):
        lidx = jnp.full((1,), layer, jnp.int32)
        xp, st_p = _layer(
            lidx, xp, w, rope_p, lam_inits, batch=bp, seq=lp,
            attend=lambda h3: flash_diff_attention(lidx, h3, lam_vecs, subln, lam_inits, tq=512),
            ret_state=zero_ret, conv_prefix=zero_conv, ffn_prefix=zero_ffn,
            rows=512, ret_chunk=256, nvalid=256, state_layer=False)
        xs, st_s = _layer(
            lidx, xs, w, rope_s, lam_inits, batch=bs, seq=sr,
            attend=lambda h3: decode_diff_attention(lidx, page_table, ck, cv, h3, lam_vecs, subln, lam_inits,
                                                    group=8),
            ret_state=state_ret, conv_prefix=conv_pre_s, ffn_prefix=ffn_pre_s,
            rows=sr, ret_chunk=sr, nvalid=ls, state_layer=True)
        outs_p.append(st_p)
        outs_s.append(st_s)

    one = jnp.zeros((1,), jnp.int32)
    y_prompt = rmsnorm(one, xp, norm_final.reshape(1, 1, D_MODEL), F32).reshape(bp, lp, D_MODEL)
    y_sample = rmsnorm(one, xs, norm_final.reshape(1, 1, D_MODEL), F32).reshape(bs, sr, D_MODEL)[:, :ls]

    def stack(outs, i):
        return jnp.stack([o[i] for o in outs])

    kp = stack(outs_p, 0).reshape(depth, bp, lp, 2 * DIFF_HEADS, DIFF_DH)
    vp = stack(outs_p, 1).reshape(depth, bp, lp, DIFF_HEADS, DIFF_DV)
    rp = stack(outs_p, 2)
    cp = stack(outs_p, 3)[:, :, CONV_HALO - (CONV_WIDTH - 1):]
    fp = stack(outs_p, 4)[:, :, FFN_HALO - (FFN_CONV_WIDTH - 1):]
    ks = stack(outs_s, 0)[:, :, :ls].reshape(depth, bs, ls, 2 * DIFF_HEADS, DIFF_DH)
    vs = stack(outs_s, 1)[:, :, :ls].reshape(depth, bs, ls, DIFF_HEADS, DIFF_DV)
    rs = stack(outs_s, 2)
    cs = stack(outs_s, 3)[:, :, CONV_HALO - (CONV_WIDTH - 1):]
    fs = stack(outs_s, 4)[:, :, FFN_HALO - (FFN_CONV_WIDTH - 1):]
    return (y_prompt, y_sample, kp, vp, rp, cp, fp, ks, vs, rs, cs, fs)
```

```python
import functools
import math

import jax
import jax.numpy as jnp
from jax import lax
from jax.experimental import pallas as pl
from jax.experimental.pallas import tpu as pltpu

F32 = jnp.float32
BF16 = jnp.bfloat16

D_MODEL = 2048
DEPTH = 4
PAGE_SIZE = 128
RET_HEADS = 8
RET_D = 128
DIFF_HEADS = 8
DIFF_DH = 64
DIFF_DV = 128
ROPE_THETA = 10000.0
CONV_CH = 1024
CONV_WIDTH = 31
D_FF = 5632
FFN_CONV_WIDTH = 3
EPS = 1e-6
LANES = 128
SUBLANES = 8
COL = 1024
CB_RQ, CB_RK, CB_RV, CB_RG, CB_DQ, CB_DK, CB_DV, CB_CA, CB_CB, CB_GATE = 0, 1, 2, 3, 4, 5, 6, 7, 8, 9
W_IN_COLS = 15 * COL
NEG = -1e30
VMEM_LIMIT = 56 * 1024 * 1024
SAMPLE_ROWS = 8


def _params(n_grid):
    return pltpu.CompilerParams(dimension_semantics=("arbitrary",) * n_grid,
                                vmem_limit_bytes=VMEM_LIMIT)


def _call(body, *, grid, in_specs, out_specs, out_shape, scratch=(), prefetch, args, name):
    spec = pltpu.PrefetchScalarGridSpec(num_scalar_prefetch=len(prefetch), grid=grid,
                                        in_specs=in_specs, out_specs=out_specs,
                                        scratch_shapes=list(scratch))
    return pl.pallas_call(body, grid_spec=spec, out_shape=out_shape,
                          compiler_params=_params(len(grid)), name=name)(*prefetch, *args)


def _sigmoid(x):
    return 1.0 / (1.0 + jnp.exp(-x))


def _silu(x):
    return x * _sigmoid(x)


def _rmsnorm_body(l_ref, x_ref, g_ref, o_ref):
    x = x_ref[...]
    inv = lax.rsqrt(jnp.mean(x * x, axis=-1, keepdims=True) + EPS)
    o_ref[...] = (x * inv * g_ref[...]).astype(o_ref.dtype)


def rmsnorm(lidx, x, g, out_dtype):
    m, d = x.shape
    tm = min(m, 512)
    return _call(
        _rmsnorm_body, grid=(m // tm,),
        in_specs=[pl.BlockSpec((tm, d), lambda i, l: (i, 0)),
                  pl.BlockSpec((None, 1, d), lambda i, l: (l[0], 0, 0))],
        out_specs=pl.BlockSpec((tm, d), lambda i, l: (i, 0)),
        out_shape=jax.ShapeDtypeStruct((m, d), out_dtype),
        prefetch=(lidx,), args=(x, g), name="rmsnorm")


def _rope_cols(x, cos, sin, head_dim):
    if head_dim == LANES:
        rot = pltpu.roll(x, LANES // 2, 1)
    else:
        half = head_dim // 2
        lane = lax.broadcasted_iota(jnp.int32, x.shape, 1)
        rot = jnp.where((lane % head_dim) < half, pltpu.roll(x, LANES - half, 1), pltpu.roll(x, half, 1))
    return x * cos + rot * sin


def _mm_body(l_ref, x_ref, w_ref, *rest, mode):
    acc = jnp.dot(x_ref[...], w_ref[...], preferred_element_type=F32)
    if mode == "plain":
        (o_ref,) = rest
        o_ref[...] = acc.astype(o_ref.dtype)
    elif mode == "res":
        r_ref, o_ref = rest
        o_ref[...] = r_ref[...] + acc
    else:
        c128, s128, c64, s64, o_ref = rest
        j = pl.program_id(0)
        is128 = jnp.logical_or(j == CB_RQ, j == CB_RK)
        is64 = jnp.logical_or(j == CB_DQ, j == CB_DK)

        @pl.when(is128)
        def _():
            for g in range(COL // LANES):
                cs = slice(g * LANES, (g + 1) * LANES)
                o_ref[:, cs] = _rope_cols(acc[:, cs], c128[...], s128[...], RET_D)

        @pl.when(is64)
        def _():
            for g in range(COL // LANES):
                cs = slice(g * LANES, (g + 1) * LANES)
                o_ref[:, cs] = _rope_cols(acc[:, cs], c64[...], s64[...], DIFF_DH)

        @pl.when(jnp.logical_not(jnp.logical_or(is128, is64)))
        def _():
            o_ref[...] = acc


def matmul(lidx, x, w, *, tn, mode="plain", res=None, rope=None, out_dtype=F32):
    m, k = x.shape
    n = w.shape[-1]
    tm = min(m, 512)
    grid = (n // tn, m // tm)
    in_specs = [pl.BlockSpec((tm, k), lambda j, i, l: (i, 0)),
                pl.BlockSpec((None, k, tn), lambda j, i, l: (l[0], 0, j))]
    args = [x, w]
    if mode == "res":
        in_specs.append(pl.BlockSpec((tm, tn), lambda j, i, l: (i, j)))
        args.append(res)
    elif mode == "rope":
        assert tn == COL
        nt = rope[0].shape[0] // tm
        for t in rope:
            in_specs.append(pl.BlockSpec((tm, LANES), lambda j, i, l: (i % nt, 0)))
            args.append(t)
    return _call(
        functools.partial(_mm_body, mode=mode), grid=grid, in_specs=in_specs,
        out_specs=pl.BlockSpec((tm, tn), lambda j, i, l: (i, j)),
        out_shape=jax.ShapeDtypeStruct((m, n), out_dtype),
        prefetch=(lidx,), args=args, name="mm_" + mode)


def _ret_body(l_ref, q_ref, k_ref, v_ref, g_ref, s0_ref, d_ref, xi_ref, zeta_ref, gc_ref,
              o_ref, sout_ref, s_scr, *, chunk):
    t = pl.program_id(1)

    @pl.when(t == 0)
    def _():
        s_scr[...] = s0_ref[...]

    rows = q_ref.shape[0]
    cp = d_ref.shape[-1]
    nt_dims = (((1,), (1,)), ((), ()))
    tn_dims = (((0,), (0,)), ((), ()))
    for h in range(RET_HEADS):
        cs = slice(h * RET_D, (h + 1) * RET_D)
        for c in range(rows // chunk):
            rs = slice(c * chunk, (c + 1) * chunk)
            q = q_ref[rs, cs]
            k = k_ref[rs, cs] * (RET_D ** -0.5)
            v = v_ref[rs, cs]
            if cp > chunk:
                pad = jnp.zeros((cp - chunk, RET_D), F32)
                q, k, v = (jnp.concatenate([a, pad], axis=0) for a in (q, k, v))
            qb, kb, vb = q.astype(BF16), k.astype(BF16), v.astype(BF16)
            inner = lax.dot_general(qb, kb, nt_dims, preferred_element_type=F32) * d_ref[h]
            s = s_scr[h]
            o = (jnp.dot(inner.astype(BF16), vb, preferred_element_type=F32)
                 + jnp.dot(qb, s.astype(BF16), preferred_element_type=F32) * xi_ref[h])
            kz = (k * zeta_ref[h]).astype(BF16)
            s_scr[h] = s * gc_ref[h] + lax.dot_general(kz, vb, tn_dims, preferred_element_type=F32)
            o = o[:chunk]
            o = o * lax.rsqrt(jnp.mean(o * o, axis=-1, keepdims=True) + EPS)
            o_ref[rs, cs] = (o * _silu(g_ref[rs, cs])).astype(o_ref.dtype)

    @pl.when(t == pl.num_programs(1) - 1)
    def _():
        sout_ref[...] = s_scr[...]


def _ret_consts(cp, nvalid):
    log_g = jnp.log1p(-jnp.exp2(-5.0 - jnp.arange(RET_HEADS, dtype=F32)))
    idx = jnp.arange(cp, dtype=F32)
    dist = idx[:, None] - idx[None, :]
    d = jnp.where(dist >= 0, jnp.exp(log_g[:, None, None] * jnp.maximum(dist, 0.0)), 0.0)
    xi = jnp.exp(log_g[:, None] * (idx + 1.0))[:, :, None]
    zeta = jnp.where(idx < nvalid, jnp.exp(log_g[:, None] * (nvalid - 1.0 - idx)), 0.0)[:, :, None]
    gc = jnp.exp(log_g * nvalid)[:, None, None]
    return d.astype(F32), xi.astype(F32), zeta.astype(F32), gc.astype(F32)


def retention(lidx, h3, s0, *, rows, chunk, nvalid, s0_layer):
    b, seq, _ = h3.shape
    cp = max(chunk, LANES)
    d, xi, zeta, gc = _ret_consts(cp, nvalid)

    def hspec(cb):
        return pl.BlockSpec((None, rows, COL), lambda bi, t, l: (bi, t, cb))

    if s0_layer:
        s0_spec = pl.BlockSpec((None, None, RET_HEADS, RET_D, RET_D), lambda bi, t, l: (l[0], bi, 0, 0, 0))
    else:
        s0_spec = pl.BlockSpec((None, RET_HEADS, RET_D, RET_D), lambda bi, t, l: (bi, 0, 0, 0))
    full3 = lambda a: pl.BlockSpec(a.shape, lambda bi, t, l: (0, 0, 0))
    return _call(
        functools.partial(_ret_body, chunk=chunk), grid=(b, seq // rows),
        in_specs=[hspec(CB_RQ), hspec(CB_RK), hspec(CB_RV), hspec(CB_RG), s0_spec,
                  full3(d), full3(xi), full3(zeta), full3(gc)],
        out_specs=[pl.BlockSpec((None, rows, COL), lambda bi, t, l: (bi, t, 0)),
                   pl.BlockSpec((None, RET_HEADS, RET_D, RET_D), lambda bi, t, l: (bi, 0, 0, 0))],
        out_shape=[jax.ShapeDtypeStruct((b, seq, COL), BF16),
                   jax.ShapeDtypeStruct((b, RET_HEADS, RET_D, RET_D), F32)],
        scratch=[pltpu.VMEM((RET_HEADS, RET_D, RET_D), F32)],
        prefetch=(lidx,), args=(h3, h3, h3, h3, s0, d, xi, zeta, gc), name="retention")


def _lambda(lq1, lk1, lq2, lk2, lam_init):
    a = jnp.exp(jnp.sum(lq1[...] * lk1[...], axis=-1, keepdims=True))
    b = jnp.exp(jnp.sum(lq2[...] * lk2[...], axis=-1, keepdims=True))
    return a - b + lam_init


def _diff_out(o, sub_ref, lam_init):
    o = o * lax.rsqrt(jnp.mean(o * o, axis=-1, keepdims=True) + EPS)
    return o * sub_ref[...] * (1.0 - lam_init)


def _softmax_step(s, vb, m, l, acc):
    m_new = jnp.maximum(m, jnp.max(s, axis=-1, keepdims=True))
    p = jnp.exp(s - m_new)
    alpha = jnp.exp(m - m_new)
    l = alpha * l + jnp.sum(p, axis=-1, keepdims=True)
    acc = alpha * acc + jnp.dot(p.astype(BF16), vb, preferred_element_type=F32)
    return m_new, l, acc


FLASH_RG = 16
LOG2E = math.log2(math.e)


def _flash_body(l_ref, q_ref, k_ref, v_ref, lq1, lk1, lq2, lk2, sub_ref, li_ref, o_ref,
                kb_scr, vb_scr, q2_scr, s_scr, p_scr, m_scr, a_scr, acc_scr, *, tq):
    qi = pl.program_id(2)
    nt_dims = (((1,), (1,)), ((), ()))

    @pl.when(qi == 0)
    def _():
        kb_scr[...] = k_ref[...].astype(BF16)
        vb_scr[:, 0:DIFF_DV] = v_ref[...].astype(BF16)
        vb_scr[:, DIFF_DV:2 * DIFF_DV] = jnp.ones((vb_scr.shape[0], DIFF_DV), BF16)

    q = q_ref[...] * (DIFF_DH ** -0.5 * LOG2E)
    lane = lax.broadcasted_iota(jnp.int32, q.shape, 1)
    q2_scr[0:tq] = jnp.where(lane < DIFF_DH, q, 0.0).astype(BF16)
    q2_scr[tq:2 * tq] = jnp.where(lane >= DIFF_DH, q, 0.0).astype(BF16)
    m_scr[...] = jnp.full(m_scr.shape, NEG, F32)
    acc_scr[...] = jnp.zeros(acc_scr.shape, F32)

    def scores(j, slot):
        off = pl.multiple_of(j * tq, tq)
        s_scr[slot] = lax.dot_general(q2_scr[...], kb_scr[pl.ds(off, tq), :], nt_dims,
                                      preferred_element_type=F32)

    def softmax_pv(j, slot, masked):
        off = pl.multiple_of(j * tq, tq)
        for r0 in range(0, 2 * tq, FLASH_RG):
            rs = slice(r0, r0 + FLASH_RG)
            s = s_scr[slot, rs, :]
            if masked:
                row = lax.broadcasted_iota(jnp.int32, s.shape, 0) + (r0 % tq)
                col = lax.broadcasted_iota(jnp.int32, s.shape, 1)
                s = jnp.where(col <= row, s, NEG)
            m_old = m_scr[rs]
            m_new = jnp.maximum(m_old, jnp.broadcast_to(jnp.max(s, axis=-1, keepdims=True), m_old.shape))
            p = jnp.exp2(s - pltpu.repeat(m_new, tq // LANES, axis=1))
            a_scr[rs] = jnp.exp2(m_old - m_new)
            m_scr[rs] = m_new
            p_scr[rs, :] = p.astype(BF16)
        vb = vb_scr[pl.ds(off, tq), :]
        for r0 in range(0, 2 * tq, tq):
            rs = slice(r0, r0 + tq)
            acc_scr[rs] = (pltpu.repeat(a_scr[rs], 2, axis=1) * acc_scr[rs]
                           + jnp.dot(p_scr[rs, :], vb, preferred_element_type=F32))

    scores(0, 0)

    def pair(jj, carry):
        j = 2 * jj
        scores(j + 1, 1)
        softmax_pv(j, 0, False)
        scores(j + 2, 0)
        softmax_pv(j + 1, 1, False)
        return carry

    lax.fori_loop(0, qi // 2, pair, 0)

    @pl.when(qi % 2 == 0)
    def _():
        softmax_pv(qi, 0, True)

    @pl.when(qi % 2 == 1)
    def _():
        scores(qi, 1)
        softmax_pv(qi - 1, 0, False)
        softmax_pv(qi, 1, True)

    lam_init = li_ref[l_ref[0]]
    lam = _lambda(lq1, lk1, lq2, lk2, lam_init)
    on = acc_scr[:, 0:DIFF_DV] / acc_scr[:, DIFF_DV:2 * DIFF_DV]
    o = on[:tq] - lam * on[tq:]
    o_ref[...] = _diff_out(o, sub_ref, lam_init).astype(o_ref.dtype)


def _lam_specs(nidx):
    def vec(width):
        if nidx == 3:
            return pl.BlockSpec((None, 1, width), lambda a, b, c, l: (l[0], 0, 0))
        return pl.BlockSpec((None, 1, width), lambda a, b, l, pt: (l[0], 0, 0))
    return [vec(DIFF_DH)] * 4 + [vec(DIFF_DV), pl.BlockSpec(memory_space=pltpu.SMEM)]


def flash_diff_attention(lidx, h3, lam_vecs, subln, lam_inits, *, tq):
    b, seq, _ = h3.shape
    per = COL // LANES

    def kvspec(cb):
        return pl.BlockSpec((None, seq, LANES), lambda bi, p, qi, l: (bi, 0, cb * per + p))

    return _call(
        functools.partial(_flash_body, tq=tq), grid=(b, DIFF_HEADS, seq // tq),
        in_specs=[pl.BlockSpec((None, tq, LANES), lambda bi, p, qi, l: (bi, qi, CB_DQ * per + p)),
                  kvspec(CB_DK), kvspec(CB_DV)] + _lam_specs(3),
        out_specs=pl.BlockSpec((None, tq, LANES), lambda bi, p, qi, l: (bi, qi, p)),
        out_shape=jax.ShapeDtypeStruct((b, seq, COL), BF16),
        scratch=[pltpu.VMEM((seq, LANES), BF16), pltpu.VMEM((seq, 2 * DIFF_DV), BF16),
                 pltpu.VMEM((2 * tq, LANES), BF16), pltpu.VMEM((2, 2 * tq, tq), F32),
                 pltpu.VMEM((2 * tq, tq), BF16), pltpu.VMEM((2 * tq, LANES), F32),
                 pltpu.VMEM((2 * tq, LANES), F32), pltpu.VMEM((2 * tq, 2 * DIFF_DV), F32)],
        prefetch=(lidx,), args=(h3, h3, h3, *lam_vecs, subln, lam_inits), name="flash_diff")


def _split_heads(x):
    ev = jnp.concatenate([x[:, (2 * p) * DIFF_DH:(2 * p + 1) * DIFF_DH] for p in range(DIFF_HEADS)], axis=0)
    od = jnp.concatenate([x[:, (2 * p + 1) * DIFF_DH:(2 * p + 2) * DIFF_DH] for p in range(DIFF_HEADS)], axis=0)
    return ev, od


def _decode_body(l_ref, pt_ref, *refs, group):
    k_pages = refs[:group]
    v_pages = refs[group:2 * group]
    (q_ref, kn_ref, vn_ref, lq1, lk1, lq2, lk2, sub_ref, li_ref,
     o_ref, qs_scr, m_scr, l_scr, acc_scr) = refs[2 * group:]
    s_id = pl.program_id(1)
    half = DIFF_HEADS * SAMPLE_ROWS
    nkh = PAGE_SIZE * DIFF_HEADS
    nt_dims = (((1,), (1,)), ((), ()))

    @pl.when(s_id == 0)
    def _():
        ev, od = _split_heads(q_ref[...] * (DIFF_DH ** -0.5))
        qs_scr[0] = ev.astype(BF16)
        qs_scr[1] = od.astype(BF16)
        m_scr[...] = jnp.full(m_scr.shape, NEG, F32)
        l_scr[...] = jnp.zeros(l_scr.shape, F32)
        acc_scr[...] = jnp.zeros(acc_scr.shape, F32)

    q1, q2 = qs_scr[0], qs_scr[1]

    def scores(ke, ko):
        return jnp.concatenate([lax.dot_general(q1, ke, nt_dims, preferred_element_type=F32),
                                lax.dot_general(q2, ko, nt_dims, preferred_element_type=F32)], axis=0)

    row_pair = (lax.broadcasted_iota(jnp.int32, (2 * half, nkh), 0) % half) // SAMPLE_ROWS
    col_pair = lax.broadcasted_iota(jnp.int32, (2 * half, nkh), 1) % DIFF_HEADS
    same_pair = row_pair == col_pair
    ss = []
    for kp in k_pages:
        ke = kp[:, pl.ds(0, DIFF_HEADS, stride=2), :].reshape(nkh, DIFF_DH).astype(BF16)
        ko = kp[:, pl.ds(1, DIFF_HEADS, stride=2), :].reshape(nkh, DIFF_DH).astype(BF16)
        ss.append(jnp.where(same_pair, scores(ke, ko), NEG))
    m = m_scr[...]
    m_new = m
    for s in ss:
        m_new = jnp.maximum(m_new, jnp.max(s, axis=-1, keepdims=True))
    alpha = jnp.exp(m - m_new)
    lsum = jnp.zeros_like(m)
    pv = jnp.zeros((2 * half, DIFF_DV), F32)
    for s, vp in zip(ss, v_pages):
        p = jnp.exp(s - m_new)
        lsum = lsum + jnp.sum(p, axis=-1, keepdims=True)
        pv = pv + jnp.dot(p.astype(BF16), vp[...].reshape(nkh, DIFF_DV).astype(BF16),
                          preferred_element_type=F32)
    l_scr[...] = alpha * l_scr[...] + lsum
    acc_scr[...] = alpha * acc_scr[...] + pv
    m_scr[...] = m_new

    @pl.when(s_id == pl.num_programs(1) - 1)
    def _():
        ke, ko = _split_heads(kn_ref[...])
        vn = vn_ref[...]
        vn = jnp.concatenate([vn[:, p * DIFF_DV:(p + 1) * DIFF_DV] for p in range(DIFF_HEADS)], axis=0)
        sn = scores(ke.astype(BF16), ko.astype(BF16))
        r = lax.broadcasted_iota(jnp.int32, sn.shape, 0) % half
        c = lax.broadcasted_iota(jnp.int32, sn.shape, 1)
        ok = jnp.logical_and(r // SAMPLE_ROWS == c // SAMPLE_ROWS, c % SAMPLE_ROWS <= r % SAMPLE_ROWS)
        sn = jnp.where(ok, sn, NEG)
        mf, lf, accf = _softmax_step(sn, vn.astype(BF16), m_scr[...], l_scr[...], acc_scr[...])
        lam_init = li_ref[l_ref[0]]
        lam = _lambda(lq1, lk1, lq2, lk2, lam_init)
        on = accf / lf
        o = _diff_out(on[:half] - lam * on[half:], sub_ref, lam_init)
        for hp in range(DIFF_HEADS):
            o_ref[:, hp * DIFF_DV:(hp + 1) * DIFF_DV] = (
                o[hp * SAMPLE_ROWS:(hp + 1) * SAMPLE_ROWS].astype(o_ref.dtype))


def decode_diff_attention(lidx, page_table, cache_k, cache_v, h3, lam_vecs, subln, lam_inits, *, group):
    b = h3.shape[0]
    n_pages = page_table.shape[1]
    half = DIFF_HEADS * SAMPLE_ROWS

    def page_spec(heads, width, g):
        return pl.BlockSpec((None, None, PAGE_SIZE, heads, width),
                            lambda bi, s, l, pt: (l[0], pt[bi, s * group + g], 0, 0, 0))

    def hspec(cb):
        return pl.BlockSpec((None, SAMPLE_ROWS, COL), lambda bi, s, l, pt: (bi, 0, cb))

    k_specs = [page_spec(2 * DIFF_HEADS, DIFF_DH, g) for g in range(group)]
    v_specs = [page_spec(DIFF_HEADS, DIFF_DV, g) for g in range(group)]
    return _call(
        functools.partial(_decode_body, group=group), grid=(b, n_pages // group),
        in_specs=k_specs + v_specs + [hspec(CB_DQ), hspec(CB_DK), hspec(CB_DV)] + _lam_specs(2),
        out_specs=pl.BlockSpec((None, SAMPLE_ROWS, COL), lambda bi, s, l, pt: (bi, 0, 0)),
        out_shape=jax.ShapeDtypeStruct((b, SAMPLE_ROWS, COL), BF16),
        scratch=[pltpu.VMEM((2, half, DIFF_DH), BF16), pltpu.VMEM((2 * half, 1), F32),
                 pltpu.VMEM((2 * half, 1), F32), pltpu.VMEM((2 * half, DIFF_DV), F32)],
        prefetch=(lidx, page_table),
        args=([cache_k] * group + [cache_v] * group + [h3, h3, h3] + list(lam_vecs) + [subln, lam_inits]),
        name="decode_diff")


CONV_HALO = 32
CONV_RC = 32
CONV_CW = 512


def _conv_body(l_ref, ca_ref, cb_ref, pre_ref, w_ref, b_ref, lg_ref, lb_ref, o_ref, st_ref, buf, y_scr,
               *, nvalid_last):
    t = pl.program_id(1)
    rows = ca_ref.shape[0]

    @pl.when(t == 0)
    def _():
        buf[0:CONV_HALO] = pre_ref[...]

    if rows >= CONV_HALO:
        @pl.when(t > 0)
        def _():
            buf[0:CONV_HALO] = buf[rows:rows + CONV_HALO]

    buf[CONV_HALO:CONV_HALO + rows] = ca_ref[...] * _sigmoid(cb_ref[...])
    base = CONV_HALO - (CONV_WIDTH - 1)
    rc = min(rows, CONV_RC)
    for c0 in range(0, CONV_CH, CONV_CW):
        cs = slice(c0, c0 + CONV_CW)
        for r0 in range(0, rows, rc):
            acc = jnp.zeros((rc, CONV_CW), F32) + b_ref[:, cs]
            for ph in range(SUBLANES):
                taps = [j for j in range(CONV_WIDTH) if (base + j) % SUBLANES == ph]
                n = rc + (SUBLANES if ph else 0)
                part = None
                for j in taps:
                    start = r0 + base + j - ph
                    term = buf[start:start + n, cs] * w_ref[j:j + 1, cs]
                    part = term if part is None else part + term
                acc = acc + part[ph:ph + rc]
            y_scr[r0:r0 + rc, cs] = acc
    y = y_scr[...]
    mu = jnp.mean(y, axis=-1, keepdims=True)
    yc = y - mu
    var = jnp.mean(yc * yc, axis=-1, keepdims=True)
    y = yc * lax.rsqrt(var + EPS) * lg_ref[...] + lb_ref[...]
    o_ref[...] = _silu(y).astype(o_ref.dtype)

    @pl.when(t == pl.num_programs(1) - 1)
    def _():
        st_ref[...] = buf[nvalid_last:nvalid_last + CONV_HALO]


def conv_module(lidx, h3, prefix, conv_w, conv_b, ln_g, ln_b, *, rows, nvalid_last, prefix_layer):
    b, seq, _ = h3.shape
    if prefix_layer:
        pre_spec = pl.BlockSpec((None, None, CONV_HALO, CONV_CH), lambda bi, t, l: (l[0], bi, 0, 0))
    else:
        pre_spec = pl.BlockSpec((None, CONV_HALO, CONV_CH), lambda bi, t, l: (bi, 0, 0))
    vec = pl.BlockSpec((None, 1, CONV_CH), lambda bi, t, l: (l[0], 0, 0))
    return _call(
        functools.partial(_conv_body, nvalid_last=nvalid_last), grid=(b, seq // rows),
        in_specs=[pl.BlockSpec((None, rows, COL), lambda bi, t, l: (bi, t, CB_CA)),
                  pl.BlockSpec((None, rows, COL), lambda bi, t, l: (bi, t, CB_CB)),
                  pre_spec,
                  pl.BlockSpec((None, CONV_WIDTH, CONV_CH), lambda bi, t, l: (l[0], 0, 0)),
                  vec, vec, vec],
        out_specs=[pl.BlockSpec((None, rows, CONV_CH), lambda bi, t, l: (bi, t, 0)),
                   pl.BlockSpec((None, CONV_HALO, CONV_CH), lambda bi, t, l: (bi, 0, 0))],
        out_shape=[jax.ShapeDtypeStruct((b, seq, CONV_CH), BF16),
                   jax.ShapeDtypeStruct((b, CONV_HALO, CONV_CH), F32)],
        scratch=[pltpu.VMEM((CONV_HALO + rows, CONV_CH), F32), pltpu.VMEM((rows, CONV_CH), F32)],
        prefetch=(lidx,), args=(h3, h3, prefix, conv_w, conv_b, ln_g, ln_b), name="conv_module")


def _merge_body(l_ref, ro_ref, do_ref, co_ref, w0, w1, w2, g0, g1, g2, o_ref):
    acc = _sigmoid(g0[...]) * jnp.dot(ro_ref[...], w0[...], preferred_element_type=F32)
    acc = acc + _sigmoid(g1[...]) * jnp.dot(do_ref[...], w1[...], preferred_element_type=F32)
    acc = acc + _sigmoid(g2[...]) * jnp.dot(co_ref[...], w2[...], preferred_element_type=F32)
    o_ref[...] = acc.astype(o_ref.dtype)


def merge(lidx, ro, do, co, h, w_branch):
    m = ro.shape[0]
    tm = min(m, 512)
    tn = COL
    nb = D_MODEL // tn
    act = pl.BlockSpec((tm, COL), lambda j, i, l: (i, 0))

    def wspec(r):
        return pl.BlockSpec((None, COL, tn), lambda j, i, l: (l[0], r, j))

    def gspec(r):
        return pl.BlockSpec((tm, tn), lambda j, i, l: (i, CB_GATE + r * nb + j))

    return _call(
        _merge_body, grid=(nb, m // tm),
        in_specs=[act, act, act, wspec(0), wspec(1), wspec(2), gspec(0), gspec(1), gspec(2)],
        out_specs=pl.BlockSpec((tm, tn), lambda j, i, l: (i, j)),
        out_shape=jax.ShapeDtypeStruct((m, D_MODEL), BF16),
        prefetch=(lidx,), args=(ro, do, co, w_branch, w_branch, w_branch, h, h, h), name="merge")


FFN_TN = 512
FFN_HALO = 8


def _ffn_gate_body(l_ref, ug_ref, uv_ref, pg_ref, pv_ref, wg_ref, wv_ref, bg_ref, bv_ref,
                   o_ref, sg_ref, sv_ref, buf, *, nvalid_last):
    t = pl.program_id(2)
    rows = ug_ref.shape[0]
    ys = []
    for half, (u_ref, p_ref, w_ref, b_ref, s_ref) in enumerate(
            ((ug_ref, pg_ref, wg_ref, bg_ref, sg_ref), (uv_ref, pv_ref, wv_ref, bv_ref, sv_ref))):
        @pl.when(t == 0)
        def _():
            buf[half, 0:FFN_HALO] = p_ref[...]

        if rows >= FFN_HALO:
            @pl.when(t > 0)
            def _():
                buf[half, 0:FFN_HALO] = buf[half, rows:rows + FFN_HALO]

        buf[half, FFN_HALO:FFN_HALO + rows] = u_ref[...]
        base = FFN_HALO - (FFN_CONV_WIDTH - 1)
        y = b_ref[...] + buf[half, base:base + rows] * w_ref[0:1]
        for j in range(1, FFN_CONV_WIDTH):
            y = y + buf[half, base + j:base + j + rows] * w_ref[j:j + 1]
        ys.append(y)

        @pl.when(t == pl.num_programs(2) - 1)
        def _():
            s_ref[...] = buf[half, nvalid_last:nvalid_last + FFN_HALO]

    o_ref[...] = (_silu(ys[0]) * ys[1]).astype(o_ref.dtype)


def ffn_gate(lidx, u3, prefix, dw_w, dw_b, *, rows, nvalid_last, prefix_layer):
    b, seq, _ = u3.shape
    nc = D_FF // FFN_TN

    def pspec(off):
        if prefix_layer:
            return pl.BlockSpec((None, None, FFN_HALO, FFN_TN), lambda bi, c, t, l: (l[0], bi, 0, c + off))
        return pl.BlockSpec((None, FFN_HALO, FFN_TN), lambda bi, c, t, l: (bi, 0, c + off))

    def uspec(off):
        return pl.BlockSpec((None, rows, FFN_TN), lambda bi, c, t, l: (bi, t, c + off))

    def wspec(off):
        return pl.BlockSpec((None, FFN_CONV_WIDTH, FFN_TN), lambda bi, c, t, l: (l[0], 0, c + off))

    def bspec(off):
        return pl.BlockSpec((None, 1, FFN_TN), lambda bi, c, t, l: (l[0], 0, c + off))

    st_spec = pl.BlockSpec((None, FFN_HALO, FFN_TN), lambda bi, c, t, l: (bi, 0, c))
    st_shape = jax.ShapeDtypeStruct((b, FFN_HALO, D_FF), F32)
    return _call(
        functools.partial(_ffn_gate_body, nvalid_last=nvalid_last), grid=(b, nc, seq // rows),
        in_specs=[uspec(0), uspec(nc), pspec(0), pspec(nc), wspec(0), wspec(nc), bspec(0), bspec(nc)],
        out_specs=[pl.BlockSpec((None, rows, FFN_TN), lambda bi, c, t, l: (bi, t, c)), st_spec, st_spec],
        out_shape=[jax.ShapeDtypeStruct((b, seq, D_FF), BF16), st_shape, st_shape],
        scratch=[pltpu.VMEM((2, FFN_HALO + rows, FFN_TN), F32)],
        prefetch=(lidx,), args=(u3, u3, prefix, prefix, dw_w, dw_w, dw_b, dw_b), name="ffn_gate")


def _up_gate_body(l_ref, x_ref, wg_ref, wv_ref, pg_ref, pv_ref, cg_ref, cv_ref, bg_ref, bv_ref,
                  o_ref, sg_ref, sv_ref, buf, *, nt, split):
    t = pl.program_id(1) % nt
    tm = x_ref.shape[0]
    hm = tm // split
    base = FFN_HALO - (FFN_CONV_WIDTH - 1)
    halves = ((wg_ref, pg_ref, cg_ref, bg_ref, sg_ref), (wv_ref, pv_ref, cv_ref, bv_ref, sv_ref))

    @pl.when(jnp.logical_and(pl.program_id(0) == 0, pl.program_id(1) == 0))
    def _():
        buf[...] = jnp.zeros(buf.shape, F32)

    tails = [jnp.where(t == 0, p_ref[...], buf[half]) for half, (_, p_ref, _, _, _) in enumerate(halves)]
    for r in range(split):
        ys = []
        for half, (w_ref, p_ref, c_ref, b_ref, s_ref) in enumerate(halves):
            u = jnp.dot(x_ref[r * hm:(r + 1) * hm, :], w_ref[...], preferred_element_type=F32)
            ext = jnp.concatenate([tails[half], u], axis=0)
            y = b_ref[...] + u * c_ref[FFN_CONV_WIDTH - 1:FFN_CONV_WIDTH]
            for j in range(FFN_CONV_WIDTH - 1):
                y = y + ext[base + j:base + j + hm] * c_ref[j:j + 1]
            ys.append(y)
            tails[half] = u[hm - FFN_HALO:]
        o_ref[r * hm:(r + 1) * hm, :] = (_silu(ys[0]) * ys[1]).astype(o_ref.dtype)
    for half, (w_ref, p_ref, c_ref, b_ref, s_ref) in enumerate(halves):
        buf[half] = tails[half]
        s_ref[...] = tails[half]


def ffn_up_gate(lidx, xn, w_up, prefix, dw_w, dw_b, *, batch, seq, tm, split):
    m, k = xn.shape
    nc = D_FF // FFN_TN
    nt = seq // tm

    def wspec(off):
        return pl.BlockSpec((None, k, FFN_TN), lambda c, i, l: (l[0], 0, c + off))

    def pspec(off):
        return pl.BlockSpec((None, FFN_HALO, FFN_TN), lambda c, i, l: (i // nt, 0, c + off))

    def cspec(off):
        return pl.BlockSpec((None, FFN_CONV_WIDTH, FFN_TN), lambda c, i, l: (l[0], 0, c + off))

    def bspec(off):
        return pl.BlockSpec((None, 1, FFN_TN), lambda c, i, l: (l[0], 0, c + off))

    st_spec = pl.BlockSpec((None, FFN_HALO, FFN_TN), lambda c, i, l: (i // nt, 0, c))
    st_shape = jax.ShapeDtypeStruct((batch, FFN_HALO, D_FF), F32)
    return _call(
        functools.partial(_up_gate_body, nt=nt, split=split), grid=(nc, m // tm),
        in_specs=[pl.BlockSpec((tm, k), lambda c, i, l: (i, 0)), wspec(0), wspec(nc), pspec(0), pspec(nc),
                  cspec(0), cspec(nc), bspec(0), bspec(nc)],
        out_specs=[pl.BlockSpec((tm, FFN_TN), lambda c, i, l: (i, c)), st_spec, st_spec],
        out_shape=[jax.ShapeDtypeStruct((m, D_FF), BF16), st_shape, st_shape],
        scratch=[pltpu.VMEM((2, FFN_HALO, FFN_TN), F32)],
        prefetch=(lidx,), args=(xn, w_up, w_up, prefix, prefix, dw_w, dw_w, dw_b, dw_b), name="ffn_up_gate")


def _rope_tables(pos, head_dim):
    half = head_dim // 2
    inv = ROPE_THETA ** (-jnp.arange(half, dtype=F32) * 2.0 / head_dim)
    ang = pos.astype(F32)[:, None] * inv[None, :]
    cos, sin = jnp.cos(ang), jnp.sin(ang)
    reps = LANES // head_dim
    return (jnp.tile(jnp.concatenate([cos, cos], axis=-1), (1, reps)),
            jnp.tile(jnp.concatenate([-sin, sin], axis=-1), (1, reps)))


def _layer(lidx, x, w, rope, lam_inits, *, batch, seq, attend, ret_state, conv_prefix, ffn_prefix,
           rows, ret_chunk, nvalid, state_layer):
    m = batch * seq
    xn = rmsnorm(lidx, x, w["norm_mix"], BF16)
    h = matmul(lidx, xn, w["w_in"], tn=COL, mode="rope", rope=rope)
    h3 = h.reshape(batch, seq, W_IN_COLS)
    ro, ret_new = retention(lidx, h3, ret_state, rows=rows, chunk=ret_chunk, nvalid=nvalid, s0_layer=state_layer)
    do = attend(h3)
    co, conv_new = conv_module(lidx, h3, conv_prefix, w["conv_w"], w["conv_b"], w["conv_ln_g"], w["conv_ln_b"],
                               rows=rows, nvalid_last=nvalid if seq == rows else rows,
                               prefix_layer=state_layer)
    mg = merge(lidx, ro.reshape(m, COL), do.reshape(m, COL), co.reshape(m, COL), h, w["w_branch"])
    x = matmul(lidx, mg, w["w_o"], tn=COL, mode="res", res=x)
    xn = rmsnorm(lidx, x, w["norm_ffn"], BF16)
    if seq > rows:
        a, st_g, st_v = ffn_up_gate(lidx, xn, w["ffn_up"], ffn_prefix, w["ffn_dw_w"], w["ffn_dw_b"],
                                    batch=batch, seq=seq, tm=rows, split=4)
    else:
        u = matmul(lidx, xn, w["ffn_up"], tn=COL)
        a, st_g, st_v = ffn_gate(lidx, u.reshape(batch, seq, 2 * D_FF), ffn_prefix, w["ffn_dw_w"], w["ffn_dw_b"],
                                 rows=rows, nvalid_last=nvalid, prefix_layer=state_layer)
    x = matmul(lidx, a.reshape(m, D_FF), w["ffn_down"], tn=FFN_TN, mode="res", res=x)
    k_rows = h3[:, :, CB_DK * COL:(CB_DK + 1) * COL]
    v_rows = h3[:, :, CB_DV * COL:(CB_DV + 1) * COL]
    ffn_new = jnp.concatenate([st_g, st_v], axis=-1)
    return x, (k_rows, v_rows, ret_new, conv_new, ffn_new)


def kernel(x_prompt, x_sample, cache_k, cache_v, page_table, state_ret, state_conv, state_ffn, norm_mix, w_in, lambda_q1, lambda_k1, lambda_q2, lambda_k2, diff_subln, conv_w, conv_b, conv_ln_g, conv_ln_b, w_branch, w_o, norm_ffn, ffn_up, ffn_dw_w, ffn_dw_b, ffn_down, norm_final):
    bp, lp, _ = x_prompt.shape
    bs, ls, _ = x_sample.shape
    depth = w_in.shape[0]
    n_pool = cache_k.shape[1]
    past_len = page_table.shape[1] * PAGE_SIZE
    sr = SAMPLE_ROWS

    vec3 = lambda a: a.reshape(a.shape[0], 1, a.shape[-1])
    w = dict(norm_mix=vec3(norm_mix), w_in=w_in.astype(BF16), conv_w=conv_w, conv_b=vec3(conv_b),
             conv_ln_g=vec3(conv_ln_g), conv_ln_b=vec3(conv_ln_b), w_branch=w_branch.astype(BF16),
             w_o=w_o.astype(BF16), norm_ffn=vec3(norm_ffn), ffn_up=ffn_up.astype(BF16),
             ffn_dw_w=ffn_dw_w, ffn_dw_b=vec3(ffn_dw_b), ffn_down=ffn_down.astype(BF16))
    lam_vecs = tuple(vec3(a) for a in (lambda_q1, lambda_k1, lambda_q2, lambda_k2))
    subln = vec3(diff_subln)
    lam_inits = jnp.asarray([0.8 - 0.6 * math.exp(-0.3 * l) for l in range(depth)], F32)

    pos_p = jnp.arange(lp, dtype=jnp.int32)
    pos_s = past_len + jnp.arange(sr, dtype=jnp.int32)
    rope_p = _rope_tables(pos_p, RET_D) + _rope_tables(pos_p, DIFF_DH)
    rope_s = tuple(jnp.tile(t, (bs, 1)) for t in _rope_tables(pos_s, RET_D) + _rope_tables(pos_s, DIFF_DH))

    conv_pre_s = jnp.pad(state_conv, ((0, 0), (0, 0), (CONV_HALO - (CONV_WIDTH - 1), 0), (0, 0)))
    ffn_pre_s = jnp.pad(state_ffn, ((0, 0), (0, 0), (FFN_HALO - (FFN_CONV_WIDTH - 1), 0), (0, 0)))
    zero_ret = jnp.zeros((bp, RET_HEADS, RET_D, RET_D), F32)
    zero_conv = jnp.zeros((bp, CONV_HALO, CONV_CH), F32)
    zero_ffn = jnp.zeros((bp, FFN_HALO, 2 * D_FF), F32)

    xp = x_prompt.reshape(bp * lp, D_MODEL)
    xs = jnp.pad(x_sample, ((0, 0), (0, sr - ls), (0, 0))).reshape(bs * sr, D_MODEL)

    outs_p, outs_s = [], []
    for layer in range(depth):
        lidx = jnp.full((1,), layer, jnp.int32)
        xp, st_p = _layer(
            lidx, xp, w, rope_p, lam_inits, batch=bp, seq=lp,
            attend=lambda h3: flash_diff_attention(lidx, h3, lam_vecs, subln, lam_inits, tq=512),
            ret_state=zero_ret, conv_prefix=zero_conv, ffn_prefix=zero_ffn,
            rows=512, ret_chunk=256, nvalid=256, state_layer=False)
        xs, st_s = _layer(
            lidx, xs, w, rope_s, lam_inits, batch=bs, seq=sr,
            attend=lambda h3: decode_diff_attention(lidx, page_table, cache_k, cache_v, h3, lam_vecs, subln,
                                                    lam_inits, group=8),
            ret_state=state_ret, conv_prefix=conv_pre_s, ffn_prefix=ffn_pre_s,
            rows=sr, ret_chunk=sr, nvalid=ls, state_layer=True)
        outs_p.append(st_p)
        outs_s.append(st_s)

    one = jnp.zeros((1,), jnp.int32)
    y_prompt = rmsnorm(one, xp, norm_final.reshape(1, 1, D_MODEL), F32).reshape(bp, lp, D_MODEL)
    y_sample = rmsnorm(one, xs, norm_final.reshape(1, 1, D_MODEL), F32).reshape(bs, sr, D_MODEL)[:, :ls]

    def stack(outs, i):
        return jnp.stack([o[i] for o in outs])

    kp = stack(outs_p, 0).reshape(depth, bp, lp, 2 * DIFF_HEADS, DIFF_DH)
    vp = stack(outs_p, 1).reshape(depth, bp, lp, DIFF_HEADS, DIFF_DV)
    rp = stack(outs_p, 2)
    cp = stack(outs_p, 3)[:, :, CONV_HALO - (CONV_WIDTH - 1):]
    fp = stack(outs_p, 4)[:, :, FFN_HALO - (FFN_CONV_WIDTH - 1):]
    ks = stack(outs_s, 0)[:, :, :ls].reshape(depth, bs, ls, 2 * DIFF_HEADS, DIFF_DH)
    vs = stack(outs_s, 1)[:, :, :ls].reshape(depth, bs, ls, DIFF_HEADS, DIFF_DV)
    rs = stack(outs_s, 2)
    cs = stack(outs_s, 3)[:, :, CONV_HALO - (CONV_WIDTH - 1):]
    fs = stack(outs_s, 4)[:, :, FFN_HALO - (FFN_CONV_WIDTH - 1):]
    return (y_prompt, y_sample, kp, vp, rp, cp, fp, ks, vs, rs, cs, fs)
```

```python
import functools
import math

import jax
import jax.numpy as jnp
from jax import lax
from jax.experimental import pallas as pl
from jax.experimental.pallas import tpu as pltpu

F32 = jnp.float32
BF16 = jnp.bfloat16

D_MODEL = 2048
DEPTH = 4
PAGE_SIZE = 128
RET_HEADS = 8
RET_D = 128
DIFF_HEADS = 8
DIFF_DH = 64
DIFF_DV = 128
ROPE_THETA = 10000.0
CONV_CH = 1024
CONV_WIDTH = 31
D_FF = 5632
FFN_CONV_WIDTH = 3
EPS = 1e-6
LANES = 128
SUBLANES = 8
COL = 1024
CB_RQ, CB_RK, CB_RV, CB_RG, CB_DQ, CB_DK, CB_DV, CB_CA, CB_CB, CB_GATE = 0, 1, 2, 3, 4, 5, 6, 7, 8, 9
W_IN_COLS = 15 * COL
NEG = -1e30
VMEM_LIMIT = 56 * 1024 * 1024
SAMPLE_ROWS = 8


def _params(n_grid):
    return pltpu.CompilerParams(dimension_semantics=("arbitrary",) * n_grid,
                                vmem_limit_bytes=VMEM_LIMIT)


def _call(body, *, grid, in_specs, out_specs, out_shape, scratch=(), prefetch, args, name):
    spec = pltpu.PrefetchScalarGridSpec(num_scalar_prefetch=len(prefetch), grid=grid,
                                        in_specs=in_specs, out_specs=out_specs,
                                        scratch_shapes=list(scratch))
    return pl.pallas_call(body, grid_spec=spec, out_shape=out_shape,
                          compiler_params=_params(len(grid)), name=name)(*prefetch, *args)


def _sigmoid(x):
    return 1.0 / (1.0 + jnp.exp(-x))


def _silu(x):
    return x * _sigmoid(x)


def _rmsnorm_body(l_ref, x_ref, g_ref, o_ref):
    x = x_ref[...]
    inv = lax.rsqrt(jnp.mean(x * x, axis=-1, keepdims=True) + EPS)
    o_ref[...] = (x * inv * g_ref[...]).astype(o_ref.dtype)


def rmsnorm(lidx, x, g, out_dtype):
    m, d = x.shape
    tm = min(m, 512)
    return _call(
        _rmsnorm_body, grid=(m // tm,),
        in_specs=[pl.BlockSpec((tm, d), lambda i, l: (i, 0)),
                  pl.BlockSpec((None, 1, d), lambda i, l: (l[0], 0, 0))],
        out_specs=pl.BlockSpec((tm, d), lambda i, l: (i, 0)),
        out_shape=jax.ShapeDtypeStruct((m, d), out_dtype),
        prefetch=(lidx,), args=(x, g), name="rmsnorm")


def _rope_cols(x, cos, sin, head_dim):
    if head_dim == LANES:
        rot = pltpu.roll(x, LANES // 2, 1)
    else:
        half = head_dim // 2
        lane = lax.broadcasted_iota(jnp.int32, x.shape, 1)
        rot = jnp.where((lane % head_dim) < half, pltpu.roll(x, LANES - half, 1), pltpu.roll(x, half, 1))
    return x * cos + rot * sin


def _mm_body(l_ref, x_ref, w_ref, *rest, mode):
    acc = jnp.dot(x_ref[...], w_ref[...], preferred_element_type=F32)
    if mode == "plain":
        (o_ref,) = rest
        o_ref[...] = acc.astype(o_ref.dtype)
    elif mode == "res":
        r_ref, o_ref = rest
        o_ref[...] = r_ref[...] + acc
    else:
        c128, s128, c64, s64, o_ref = rest
        j = pl.program_id(0)
        is128 = jnp.logical_or(j == CB_RQ, j == CB_RK)
        is64 = jnp.logical_or(j == CB_DQ, j == CB_DK)

        @pl.when(is128)
        def _():
            for g in range(COL // LANES):
                cs = slice(g * LANES, (g + 1) * LANES)
                o_ref[:, cs] = _rope_cols(acc[:, cs], c128[...], s128[...], RET_D)

        @pl.when(is64)
        def _():
            for g in range(COL // LANES):
                cs = slice(g * LANES, (g + 1) * LANES)
                o_ref[:, cs] = _rope_cols(acc[:, cs], c64[...], s64[...], DIFF_DH)

        @pl.when(jnp.logical_not(jnp.logical_or(is128, is64)))
        def _():
            o_ref[...] = acc


def matmul(lidx, x, w, *, tn, mode="plain", res=None, rope=None, out_dtype=F32):
    m, k = x.shape
    n = w.shape[-1]
    tm = min(m, 512)
    grid = (n // tn, m // tm)
    in_specs = [pl.BlockSpec((tm, k), lambda j, i, l: (i, 0)),
                pl.BlockSpec((None, k, tn), lambda j, i, l: (l[0], 0, j))]
    args = [x, w]
    if mode == "res":
        in_specs.append(pl.BlockSpec((tm, tn), lambda j, i, l: (i, j)))
        args.append(res)
    elif mode == "rope":
        assert tn == COL
        nt = rope[0].shape[0] // tm
        for t in rope:
            in_specs.append(pl.BlockSpec((tm, LANES), lambda j, i, l: (i % nt, 0)))
            args.append(t)
    return _call(
        functools.partial(_mm_body, mode=mode), grid=grid, in_specs=in_specs,
        out_specs=pl.BlockSpec((tm, tn), lambda j, i, l: (i, j)),
        out_shape=jax.ShapeDtypeStruct((m, n), out_dtype),
        prefetch=(lidx,), args=args, name="mm_" + mode)


def _ret_body(l_ref, q_ref, k_ref, v_ref, g_ref, s0_ref, d_ref, xi_ref, zeta_ref, gc_ref,
              o_ref, sout_ref, s_scr, *, chunk):
    t = pl.program_id(1)

    @pl.when(t == 0)
    def _():
        s_scr[...] = s0_ref[...]

    rows = q_ref.shape[0]
    cp = d_ref.shape[-1]
    nt_dims = (((1,), (1,)), ((), ()))
    tn_dims = (((0,), (0,)), ((), ()))
    for h in range(RET_HEADS):
        cs = slice(h * RET_D, (h + 1) * RET_D)
        for c in range(rows // chunk):
            rs = slice(c * chunk, (c + 1) * chunk)
            q = q_ref[rs, cs]
            k = k_ref[rs, cs] * (RET_D ** -0.5)
            v = v_ref[rs, cs]
            if cp > chunk:
                pad = jnp.zeros((cp - chunk, RET_D), F32)
                q, k, v = (jnp.concatenate([a, pad], axis=0) for a in (q, k, v))
            qb, kb, vb = q.astype(BF16), k.astype(BF16), v.astype(BF16)
            inner = lax.dot_general(qb, kb, nt_dims, preferred_element_type=F32) * d_ref[h]
            s = s_scr[h]
            o = (jnp.dot(inner.astype(BF16), vb, preferred_element_type=F32)
                 + jnp.dot(qb, s.astype(BF16), preferred_element_type=F32) * xi_ref[h])
            kz = (k * zeta_ref[h]).astype(BF16)
            s_scr[h] = s * gc_ref[h] + lax.dot_general(kz, vb, tn_dims, preferred_element_type=F32)
            o = o[:chunk]
            o = o * lax.rsqrt(jnp.mean(o * o, axis=-1, keepdims=True) + EPS)
            o_ref[rs, cs] = (o * _silu(g_ref[rs, cs])).astype(o_ref.dtype)

    @pl.when(t == pl.num_programs(1) - 1)
    def _():
        sout_ref[...] = s_scr[...]


def _ret_consts(cp, nvalid):
    log_g = jnp.log1p(-jnp.exp2(-5.0 - jnp.arange(RET_HEADS, dtype=F32)))
    idx = jnp.arange(cp, dtype=F32)
    dist = idx[:, None] - idx[None, :]
    d = jnp.where(dist >= 0, jnp.exp(log_g[:, None, None] * jnp.maximum(dist, 0.0)), 0.0)
    xi = jnp.exp(log_g[:, None] * (idx + 1.0))[:, :, None]
    zeta = jnp.where(idx < nvalid, jnp.exp(log_g[:, None] * (nvalid - 1.0 - idx)), 0.0)[:, :, None]
    gc = jnp.exp(log_g * nvalid)[:, None, None]
    return d.astype(F32), xi.astype(F32), zeta.astype(F32), gc.astype(F32)


def retention(lidx, h3, s0, *, rows, chunk, nvalid, s0_layer):
    b, seq, _ = h3.shape
    cp = max(chunk, LANES)
    d, xi, zeta, gc = _ret_consts(cp, nvalid)

    def hspec(cb):
        return pl.BlockSpec((None, rows, COL), lambda bi, t, l: (bi, t, cb))

    if s0_layer:
        s0_spec = pl.BlockSpec((None, None, RET_HEADS, RET_D, RET_D), lambda bi, t, l: (l[0], bi, 0, 0, 0))
    else:
        s0_spec = pl.BlockSpec((None, RET_HEADS, RET_D, RET_D), lambda bi, t, l: (bi, 0, 0, 0))
    full3 = lambda a: pl.BlockSpec(a.shape, lambda bi, t, l: (0, 0, 0))
    return _call(
        functools.partial(_ret_body, chunk=chunk), grid=(b, seq // rows),
        in_specs=[hspec(CB_RQ), hspec(CB_RK), hspec(CB_RV), hspec(CB_RG), s0_spec,
                  full3(d), full3(xi), full3(zeta), full3(gc)],
        out_specs=[pl.BlockSpec((None, rows, COL), lambda bi, t, l: (bi, t, 0)),
                   pl.BlockSpec((None, RET_HEADS, RET_D, RET_D), lambda bi, t, l: (bi, 0, 0, 0))],
        out_shape=[jax.ShapeDtypeStruct((b, seq, COL), BF16),
                   jax.ShapeDtypeStruct((b, RET_HEADS, RET_D, RET_D), F32)],
        scratch=[pltpu.VMEM((RET_HEADS, RET_D, RET_D), F32)],
        prefetch=(lidx,), args=(h3, h3, h3, h3, s0, d, xi, zeta, gc), name="retention")


def _lambda(lq1, lk1, lq2, lk2, lam_init):
    a = jnp.exp(jnp.sum(lq1[...] * lk1[...], axis=-1, keepdims=True))
    b = jnp.exp(jnp.sum(lq2[...] * lk2[...], axis=-1, keepdims=True))
    return a - b + lam_init


def _diff_out(o, sub_ref, lam_init):
    o = o * lax.rsqrt(jnp.mean(o * o, axis=-1, keepdims=True) + EPS)
    return o * sub_ref[...] * (1.0 - lam_init)


def _softmax_step(s, vb, m, l, acc):
    m_new = jnp.maximum(m, jnp.max(s, axis=-1, keepdims=True))
    p = jnp.exp(s - m_new)
    alpha = jnp.exp(m - m_new)
    l = alpha * l + jnp.sum(p, axis=-1, keepdims=True)
    acc = alpha * acc + jnp.dot(p.astype(BF16), vb, preferred_element_type=F32)
    return m_new, l, acc


FLASH_RG = 16
LOG2E = math.log2(math.e)


def _flash_body(l_ref, q_ref, k_ref, v_ref, lq1, lk1, lq2, lk2, sub_ref, li_ref, o_ref,
                kb_scr, vb_scr, q2_scr, s_scr, p_scr, m_scr, a_scr, acc_scr, *, tq):
    qi = pl.program_id(2)
    nt_dims = (((1,), (1,)), ((), ()))

    @pl.when(qi == 0)
    def _():
        kb_scr[...] = k_ref[...].astype(BF16)
        vb_scr[:, 0:DIFF_DV] = v_ref[...].astype(BF16)
        vb_scr[:, DIFF_DV:2 * DIFF_DV] = jnp.ones((vb_scr.shape[0], DIFF_DV), BF16)

    q = q_ref[...] * (DIFF_DH ** -0.5 * LOG2E)
    lane = lax.broadcasted_iota(jnp.int32, q.shape, 1)
    q2_scr[0:tq] = jnp.where(lane < DIFF_DH, q, 0.0).astype(BF16)
    q2_scr[tq:2 * tq] = jnp.where(lane >= DIFF_DH, q, 0.0).astype(BF16)
    m_scr[...] = jnp.full(m_scr.shape, NEG, F32)
    acc_scr[...] = jnp.zeros(acc_scr.shape, F32)

    def scores(j, slot):
        off = pl.multiple_of(j * tq, tq)
        s_scr[slot] = lax.dot_general(q2_scr[...], kb_scr[pl.ds(off, tq), :], nt_dims,
                                      preferred_element_type=F32)

    def softmax_pv(j, slot, masked):
        off = pl.multiple_of(j * tq, tq)
        for r0 in range(0, 2 * tq, FLASH_RG):
            rs = slice(r0, r0 + FLASH_RG)
            s = s_scr[slot, rs, :]
            if masked:
                row = lax.broadcasted_iota(jnp.int32, s.shape, 0) + (r0 % tq)
                col = lax.broadcasted_iota(jnp.int32, s.shape, 1)
                s = jnp.where(col <= row, s, NEG)
            m_old = m_scr[rs]
            m_new = jnp.maximum(m_old, jnp.broadcast_to(jnp.max(s, axis=-1, keepdims=True), m_old.shape))
            p = jnp.exp2(s - pltpu.repeat(m_new, tq // LANES, axis=1))
            a_scr[rs] = jnp.exp2(m_old - m_new)
            m_scr[rs] = m_new
            p_scr[rs, :] = p.astype(BF16)
        vb = vb_scr[pl.ds(off, tq), :]
        for r0 in range(0, 2 * tq, tq):
            rs = slice(r0, r0 + tq)
            acc_scr[rs] = (pltpu.repeat(a_scr[rs], 2, axis=1) * acc_scr[rs]
                           + jnp.dot(p_scr[rs, :], vb, preferred_element_type=F32))

    scores(0, 0)

    def pair(jj, carry):
        j = 2 * jj
        scores(j + 1, 1)
        softmax_pv(j, 0, False)
        scores(j + 2, 0)
        softmax_pv(j + 1, 1, False)
        return carry

    lax.fori_loop(0, qi // 2, pair, 0)

    @pl.when(qi % 2 == 0)
    def _():
        softmax_pv(qi, 0, True)

    @pl.when(qi % 2 == 1)
    def _():
        scores(qi, 1)
        softmax_pv(qi - 1, 0, False)
        softmax_pv(qi, 1, True)

    lam_init = li_ref[l_ref[0]]
    lam = _lambda(lq1, lk1, lq2, lk2, lam_init)
    on = acc_scr[:, 0:DIFF_DV] / acc_scr[:, DIFF_DV:2 * DIFF_DV]
    o = on[:tq] - lam * on[tq:]
    o_ref[...] = _diff_out(o, sub_ref, lam_init).astype(o_ref.dtype)


def _lam_specs(nidx):
    def vec(width):
        if nidx == 3:
            return pl.BlockSpec((None, 1, width), lambda a, b, c, l: (l[0], 0, 0))
        return pl.BlockSpec((None, 1, width), lambda a, b, l, pt: (l[0], 0, 0))
    return [vec(DIFF_DH)] * 4 + [vec(DIFF_DV), pl.BlockSpec(memory_space=pltpu.SMEM)]


def flash_diff_attention(lidx, h3, lam_vecs, subln, lam_inits, *, tq):
    b, seq, _ = h3.shape
    per = COL // LANES

    def kvspec(cb):
        return pl.BlockSpec((None, seq, LANES), lambda bi, p, qi, l: (bi, 0, cb * per + p))

    return _call(
        functools.partial(_flash_body, tq=tq), grid=(b, DIFF_HEADS, seq // tq),
        in_specs=[pl.BlockSpec((None, tq, LANES), lambda bi, p, qi, l: (bi, qi, CB_DQ * per + p)),
                  kvspec(CB_DK), kvspec(CB_DV)] + _lam_specs(3),
        out_specs=pl.BlockSpec((None, tq, LANES), lambda bi, p, qi, l: (bi, qi, p)),
        out_shape=jax.ShapeDtypeStruct((b, seq, COL), BF16),
        scratch=[pltpu.VMEM((seq, LANES), BF16), pltpu.VMEM((seq, 2 * DIFF_DV), BF16),
                 pltpu.VMEM((2 * tq, LANES), BF16), pltpu.VMEM((2, 2 * tq, tq), F32),
                 pltpu.VMEM((2 * tq, tq), BF16), pltpu.VMEM((2 * tq, LANES), F32),
                 pltpu.VMEM((2 * tq, LANES), F32), pltpu.VMEM((2 * tq, 2 * DIFF_DV), F32)],
        prefetch=(lidx,), args=(h3, h3, h3, *lam_vecs, subln, lam_inits), name="flash_diff")


def _decode_body(l_ref, pt_ref, *refs, group):
    k_pages = refs[:group]
    v_pages = refs[group:2 * group]
    (q_ref, kn_ref, vn_ref, lq1, lk1, lq2, lk2, sub_ref, li_ref,
     o_ref, qbd_scr, m_scr, l_scr, acc_scr) = refs[2 * group:]
    s_id = pl.program_id(1)
    pair_rows = 2 * SAMPLE_ROWS
    nt_dims = (((1,), (1,)), ((), ()))

    @pl.when(s_id == 0)
    def _():
        q = q_ref[...] * (DIFF_DH ** -0.5)
        qt = jnp.concatenate([q] * (2 * DIFF_HEADS), axis=0)
        rh = lax.broadcasted_iota(jnp.int32, qt.shape, 0) // SAMPLE_ROWS
        ch = lax.broadcasted_iota(jnp.int32, qt.shape, 1) // DIFF_DH
        qbd_scr[...] = jnp.where(rh == ch, qt, 0.0).astype(BF16)
        m_scr[...] = jnp.full(m_scr.shape, NEG, F32)
        l_scr[...] = jnp.zeros(l_scr.shape, F32)
        acc_scr[...] = jnp.zeros(acc_scr.shape, F32)

    qbd = qbd_scr[...]
    s = jnp.concatenate(
        [jnp.dot(qbd, kp[...].reshape(COL, PAGE_SIZE).astype(BF16), preferred_element_type=F32)
         for kp in k_pages], axis=1)
    m = m_scr[...]
    m_new = jnp.maximum(m, jnp.max(s, axis=-1, keepdims=True))
    p = jnp.exp(s - m_new).astype(BF16)
    alpha = jnp.exp(m - m_new)
    l_scr[...] = alpha * l_scr[...] + jnp.sum(p.astype(F32), axis=-1, keepdims=True)
    pv = []
    for hp in range(DIFF_HEADS):
        vh = jnp.concatenate([vp[pl.ds(hp, PAGE_SIZE, stride=DIFF_HEADS), :] for vp in v_pages],
                             axis=0).astype(BF16)
        pv.append(jnp.dot(p[hp * pair_rows:(hp + 1) * pair_rows], vh, preferred_element_type=F32))
    acc_scr[...] = alpha * acc_scr[...] + jnp.concatenate(pv, axis=0)
    m_scr[...] = m_new

    @pl.when(s_id == pl.num_programs(1) - 1)
    def _():
        pad = jnp.zeros((LANES - SAMPLE_ROWS, COL), F32)
        kn = jnp.concatenate([kn_ref[...], pad], axis=0).astype(BF16)
        vn = jnp.concatenate([vn_ref[...], pad], axis=0).astype(BF16)
        sn = lax.dot_general(qbd, kn, nt_dims, preferred_element_type=F32)
        tok = lax.broadcasted_iota(jnp.int32, sn.shape, 0) % SAMPLE_ROWS
        key = lax.broadcasted_iota(jnp.int32, sn.shape, 1)
        sn = jnp.where(key <= tok, sn, NEG)
        mo = m_scr[...]
        mf = jnp.maximum(mo, jnp.max(sn, axis=-1, keepdims=True))
        pn = jnp.exp(sn - mf).astype(BF16)
        af = jnp.exp(mo - mf)
        lf = af * l_scr[...] + jnp.sum(pn.astype(F32), axis=-1, keepdims=True)
        pvn = jnp.dot(pn, vn, preferred_element_type=F32)
        pvn = jnp.concatenate([pvn[hp * pair_rows:(hp + 1) * pair_rows, hp * DIFF_DV:(hp + 1) * DIFF_DV]
                               for hp in range(DIFF_HEADS)], axis=0)
        on = (af * acc_scr[...] + pvn) / lf
        lam_init = li_ref[l_ref[0]]
        lam = _lambda(lq1, lk1, lq2, lk2, lam_init)
        for hp in range(DIFF_HEADS):
            r0 = hp * pair_rows
            o = on[r0:r0 + SAMPLE_ROWS] - lam * on[r0 + SAMPLE_ROWS:r0 + pair_rows]
            o_ref[:, hp * DIFF_DV:(hp + 1) * DIFF_DV] = _diff_out(o, sub_ref, lam_init).astype(o_ref.dtype)


def decode_diff_attention(lidx, page_table, cache_kt, cache_v, h3, lam_vecs, subln, lam_inits, *, group):
    b = h3.shape[0]
    n_pages = page_table.shape[1]
    rows = 2 * DIFF_HEADS * SAMPLE_ROWS

    def page_spec(shape, g):
        zeros = (0,) * len(shape)
        return pl.BlockSpec((None, None) + shape, lambda bi, s, l, pt: (l[0], pt[bi, s * group + g]) + zeros)

    def hspec(cb):
        return pl.BlockSpec((None, SAMPLE_ROWS, COL), lambda bi, s, l, pt: (bi, 0, cb))

    k_specs = [page_spec((2 * DIFF_HEADS, DIFF_DH, PAGE_SIZE), g) for g in range(group)]
    v_specs = [page_spec((PAGE_SIZE * DIFF_HEADS, DIFF_DV), g) for g in range(group)]
    return _call(
        functools.partial(_decode_body, group=group), grid=(b, n_pages // group),
        in_specs=k_specs + v_specs + [hspec(CB_DQ), hspec(CB_DK), hspec(CB_DV)] + _lam_specs(2),
        out_specs=pl.BlockSpec((None, SAMPLE_ROWS, COL), lambda bi, s, l, pt: (bi, 0, 0)),
        out_shape=jax.ShapeDtypeStruct((b, SAMPLE_ROWS, COL), BF16),
        scratch=[pltpu.VMEM((rows, COL), BF16), pltpu.VMEM((rows, 1), F32), pltpu.VMEM((rows, 1), F32),
                 pltpu.VMEM((rows, DIFF_DV), F32)],
        prefetch=(lidx, page_table),
        args=([cache_kt] * group + [cache_v] * group + [h3, h3, h3] + list(lam_vecs) + [subln, lam_inits]),
        name="decode_diff")


CONV_HALO = 32
CONV_RC = 32
CONV_CW = 512


def _conv_body(l_ref, ca_ref, cb_ref, pre_ref, w_ref, b_ref, lg_ref, lb_ref, o_ref, st_ref, buf, y_scr,
               *, nvalid_last):
    t = pl.program_id(1)
    rows = ca_ref.shape[0]

    @pl.when(t == 0)
    def _():
        buf[0:CONV_HALO] = pre_ref[...]

    if rows >= CONV_HALO:
        @pl.when(t > 0)
        def _():
            buf[0:CONV_HALO] = buf[rows:rows + CONV_HALO]

    buf[CONV_HALO:CONV_HALO + rows] = ca_ref[...] * _sigmoid(cb_ref[...])
    base = CONV_HALO - (CONV_WIDTH - 1)
    rc = min(rows, CONV_RC)
    for c0 in range(0, CONV_CH, CONV_CW):
        cs = slice(c0, c0 + CONV_CW)
        for r0 in range(0, rows, rc):
            acc = jnp.zeros((rc, CONV_CW), F32) + b_ref[:, cs]
            for ph in range(SUBLANES):
                taps = [j for j in range(CONV_WIDTH) if (base + j) % SUBLANES == ph]
                n = rc + (SUBLANES if ph else 0)
                part = None
                for j in taps:
                    start = r0 + base + j - ph
                    term = buf[start:start + n, cs] * w_ref[j:j + 1, cs]
                    part = term if part is None else part + term
                acc = acc + part[ph:ph + rc]
            y_scr[r0:r0 + rc, cs] = acc
    y = y_scr[...]
    mu = jnp.mean(y, axis=-1, keepdims=True)
    yc = y - mu
    var = jnp.mean(yc * yc, axis=-1, keepdims=True)
    y = yc * lax.rsqrt(var + EPS) * lg_ref[...] + lb_ref[...]
    o_ref[...] = _silu(y).astype(o_ref.dtype)

    @pl.when(t == pl.num_programs(1) - 1)
    def _():
        st_ref[...] = buf[nvalid_last:nvalid_last + CONV_HALO]


def conv_module(lidx, h3, prefix, conv_w, conv_b, ln_g, ln_b, *, rows, nvalid_last, prefix_layer):
    b, seq, _ = h3.shape
    if prefix_layer:
        pre_spec = pl.BlockSpec((None, None, CONV_HALO, CONV_CH), lambda bi, t, l: (l[0], bi, 0, 0))
    else:
        pre_spec = pl.BlockSpec((None, CONV_HALO, CONV_CH), lambda bi, t, l: (bi, 0, 0))
    vec = pl.BlockSpec((None, 1, CONV_CH), lambda bi, t, l: (l[0], 0, 0))
    return _call(
        functools.partial(_conv_body, nvalid_last=nvalid_last), grid=(b, seq // rows),
        in_specs=[pl.BlockSpec((None, rows, COL), lambda bi, t, l: (bi, t, CB_CA)),
                  pl.BlockSpec((None, rows, COL), lambda bi, t, l: (bi, t, CB_CB)),
                  pre_spec,
                  pl.BlockSpec((None, CONV_WIDTH, CONV_CH), lambda bi, t, l: (l[0], 0, 0)),
                  vec, vec, vec],
        out_specs=[pl.BlockSpec((None, rows, CONV_CH), lambda bi, t, l: (bi, t, 0)),
                   pl.BlockSpec((None, CONV_HALO, CONV_CH), lambda bi, t, l: (bi, 0, 0))],
        out_shape=[jax.ShapeDtypeStruct((b, seq, CONV_CH), BF16),
                   jax.ShapeDtypeStruct((b, CONV_HALO, CONV_CH), F32)],
        scratch=[pltpu.VMEM((CONV_HALO + rows, CONV_CH), F32), pltpu.VMEM((rows, CONV_CH), F32)],
        prefetch=(lidx,), args=(h3, h3, prefix, conv_w, conv_b, ln_g, ln_b), name="conv_module")


def _merge_body(l_ref, ro_ref, do_ref, co_ref, w0, w1, w2, g0, g1, g2, o_ref):
    acc = _sigmoid(g0[...]) * jnp.dot(ro_ref[...], w0[...], preferred_element_type=F32)
    acc = acc + _sigmoid(g1[...]) * jnp.dot(do_ref[...], w1[...], preferred_element_type=F32)
    acc = acc + _sigmoid(g2[...]) * jnp.dot(co_ref[...], w2[...], preferred_element_type=F32)
    o_ref[...] = acc.astype(o_ref.dtype)


def merge(lidx, ro, do, co, h, w_branch):
    m = ro.shape[0]
    tm = min(m, 512)
    tn = COL
    nb = D_MODEL // tn
    act = pl.BlockSpec((tm, COL), lambda j, i, l: (i, 0))

    def wspec(r):
        return pl.BlockSpec((None, COL, tn), lambda j, i, l: (l[0], r, j))

    def gspec(r):
        return pl.BlockSpec((tm, tn), lambda j, i, l: (i, CB_GATE + r * nb + j))

    return _call(
        _merge_body, grid=(nb, m // tm),
        in_specs=[act, act, act, wspec(0), wspec(1), wspec(2), gspec(0), gspec(1), gspec(2)],
        out_specs=pl.BlockSpec((tm, tn), lambda j, i, l: (i, j)),
        out_shape=jax.ShapeDtypeStruct((m, D_MODEL), BF16),
        prefetch=(lidx,), args=(ro, do, co, w_branch, w_branch, w_branch, h, h, h), name="merge")


FFN_TN = 512
FFN_HALO = 8


def _ffn_gate_body(l_ref, ug_ref, uv_ref, pg_ref, pv_ref, wg_ref, wv_ref, bg_ref, bv_ref,
                   o_ref, sg_ref, sv_ref, buf, *, nvalid_last):
    t = pl.program_id(2)
    rows = ug_ref.shape[0]
    ys = []
    for half, (u_ref, p_ref, w_ref, b_ref, s_ref) in enumerate(
            ((ug_ref, pg_ref, wg_ref, bg_ref, sg_ref), (uv_ref, pv_ref, wv_ref, bv_ref, sv_ref))):
        @pl.when(t == 0)
        def _():
            buf[half, 0:FFN_HALO] = p_ref[...]

        if rows >= FFN_HALO:
            @pl.when(t > 0)
            def _():
                buf[half, 0:FFN_HALO] = buf[half, rows:rows + FFN_HALO]

        buf[half, FFN_HALO:FFN_HALO + rows] = u_ref[...]
        base = FFN_HALO - (FFN_CONV_WIDTH - 1)
        y = b_ref[...] + buf[half, base:base + rows] * w_ref[0:1]
        for j in range(1, FFN_CONV_WIDTH):
            y = y + buf[half, base + j:base + j + rows] * w_ref[j:j + 1]
        ys.append(y)

        @pl.when(t == pl.num_programs(2) - 1)
        def _():
            s_ref[...] = buf[half, nvalid_last:nvalid_last + FFN_HALO]

    o_ref[...] = (_silu(ys[0]) * ys[1]).astype(o_ref.dtype)


def ffn_gate(lidx, u3, prefix, dw_w, dw_b, *, rows, nvalid_last, prefix_layer):
    b, seq, _ = u3.shape
    nc = D_FF // FFN_TN

    def pspec(off):
        if prefix_layer:
            return pl.BlockSpec((None, None, FFN_HALO, FFN_TN), lambda bi, c, t, l: (l[0], bi, 0, c + off))
        return pl.BlockSpec((None, FFN_HALO, FFN_TN), lambda bi, c, t, l: (bi, 0, c + off))

    def uspec(off):
        return pl.BlockSpec((None, rows, FFN_TN), lambda bi, c, t, l: (bi, t, c + off))

    def wspec(off):
        return pl.BlockSpec((None, FFN_CONV_WIDTH, FFN_TN), lambda bi, c, t, l: (l[0], 0, c + off))

    def bspec(off):
        return pl.BlockSpec((None, 1, FFN_TN), lambda bi, c, t, l: (l[0], 0, c + off))

    st_spec = pl.BlockSpec((None, FFN_HALO, FFN_TN), lambda bi, c, t, l: (bi, 0, c))
    st_shape = jax.ShapeDtypeStruct((b, FFN_HALO, D_FF), F32)
    return _call(
        functools.partial(_ffn_gate_body, nvalid_last=nvalid_last), grid=(b, nc, seq // rows),
        in_specs=[uspec(0), uspec(nc), pspec(0), pspec(nc), wspec(0), wspec(nc), bspec(0), bspec(nc)],
        out_specs=[pl.BlockSpec((None, rows, FFN_TN), lambda bi, c, t, l: (bi, t, c)), st_spec, st_spec],
        out_shape=[jax.ShapeDtypeStruct((b, seq, D_FF), BF16), st_shape, st_shape],
        scratch=[pltpu.VMEM((2, FFN_HALO + rows, FFN_TN), F32)],
        prefetch=(lidx,), args=(u3, u3, prefix, prefix, dw_w, dw_w, dw_b, dw_b), name="ffn_gate")


def _up_gate_body(l_ref, x_ref, wg_ref, wv_ref, pg_ref, pv_ref, cg_ref, cv_ref, bg_ref, bv_ref,
                  o_ref, sg_ref, sv_ref, buf, *, nt, split):
    t = pl.program_id(1) % nt
    tm = x_ref.shape[0]
    hm = tm // split
    base = FFN_HALO - (FFN_CONV_WIDTH - 1)
    halves = ((wg_ref, pg_ref, cg_ref, bg_ref, sg_ref), (wv_ref, pv_ref, cv_ref, bv_ref, sv_ref))

    @pl.when(jnp.logical_and(pl.program_id(0) == 0, pl.program_id(1) == 0))
    def _():
        buf[...] = jnp.zeros(buf.shape, F32)

    tails = [jnp.where(t == 0, p_ref[...], buf[half]) for half, (_, p_ref, _, _, _) in enumerate(halves)]
    for r in range(split):
        ys = []
        for half, (w_ref, p_ref, c_ref, b_ref, s_ref) in enumerate(halves):
            u = jnp.dot(x_ref[r * hm:(r + 1) * hm, :], w_ref[...], preferred_element_type=F32)
            ext = jnp.concatenate([tails[half], u], axis=0)
            y = b_ref[...] + u * c_ref[FFN_CONV_WIDTH - 1:FFN_CONV_WIDTH]
            for j in range(FFN_CONV_WIDTH - 1):
                y = y + ext[base + j:base + j + hm] * c_ref[j:j + 1]
            ys.append(y)
            tails[half] = u[hm - FFN_HALO:]
        o_ref[r * hm:(r + 1) * hm, :] = (_silu(ys[0]) * ys[1]).astype(o_ref.dtype)
    for half, (w_ref, p_ref, c_ref, b_ref, s_ref) in enumerate(halves):
        buf[half] = tails[half]
        s_ref[...] = tails[half]


def ffn_up_gate(lidx, xn, w_up, prefix, dw_w, dw_b, *, batch, seq, tm, split):
    m, k = xn.shape
    nc = D_FF // FFN_TN
    nt = seq // tm

    def wspec(off):
        return pl.BlockSpec((None, k, FFN_TN), lambda c, i, l: (l[0], 0, c + off))

    def pspec(off):
        return pl.BlockSpec((None, FFN_HALO, FFN_TN), lambda c, i, l: (i // nt, 0, c + off))

    def cspec(off):
        return pl.BlockSpec((None, FFN_CONV_WIDTH, FFN_TN), lambda c, i, l: (l[0], 0, c + off))

    def bspec(off):
        return pl.BlockSpec((None, 1, FFN_TN), lambda c, i, l: (l[0], 0, c + off))

    st_spec = pl.BlockSpec((None, FFN_HALO, FFN_TN), lambda c, i, l: (i // nt, 0, c))
    st_shape = jax.ShapeDtypeStruct((batch, FFN_HALO, D_FF), F32)
    return _call(
        functools.partial(_up_gate_body, nt=nt, split=split), grid=(nc, m // tm),
        in_specs=[pl.BlockSpec((tm, k), lambda c, i, l: (i, 0)), wspec(0), wspec(nc), pspec(0), pspec(nc),
                  cspec(0), cspec(nc), bspec(0), bspec(nc)],
        out_specs=[pl.BlockSpec((tm, FFN_TN), lambda c, i, l: (i, c)), st_spec, st_spec],
        out_shape=[jax.ShapeDtypeStruct((m, D_FF), BF16), st_shape, st_shape],
        scratch=[pltpu.VMEM((2, FFN_HALO, FFN_TN), F32)],
        prefetch=(lidx,), args=(xn, w_up, w_up, prefix, prefix, dw_w, dw_w, dw_b, dw_b), name="ffn_up_gate")


def _rope_tables(pos, head_dim):
    half = head_dim // 2
    inv = ROPE_THETA ** (-jnp.arange(half, dtype=F32) * 2.0 / head_dim)
    ang = pos.astype(F32)[:, None] * inv[None, :]
    cos, sin = jnp.cos(ang), jnp.sin(ang)
    reps = LANES // head_dim
    return (jnp.tile(jnp.concatenate([cos, cos], axis=-1), (1, reps)),
            jnp.tile(jnp.concatenate([-sin, sin], axis=-1), (1, reps)))


def _layer(lidx, x, w, rope, lam_inits, *, batch, seq, attend, ret_state, conv_prefix, ffn_prefix,
           rows, ret_chunk, nvalid, state_layer):
    m = batch * seq
    xn = rmsnorm(lidx, x, w["norm_mix"], BF16)
    h = matmul(lidx, xn, w["w_in"], tn=COL, mode="rope", rope=rope)
    h3 = h.reshape(batch, seq, W_IN_COLS)
    ro, ret_new = retention(lidx, h3, ret_state, rows=rows, chunk=ret_chunk, nvalid=nvalid, s0_layer=state_layer)
    do = attend(h3)
    co, conv_new = conv_module(lidx, h3, conv_prefix, w["conv_w"], w["conv_b"], w["conv_ln_g"], w["conv_ln_b"],
                               rows=rows, nvalid_last=nvalid if seq == rows else rows,
                               prefix_layer=state_layer)
    mg = merge(lidx, ro.reshape(m, COL), do.reshape(m, COL), co.reshape(m, COL), h, w["w_branch"])
    x = matmul(lidx, mg, w["w_o"], tn=COL, mode="res", res=x)
    xn = rmsnorm(lidx, x, w["norm_ffn"], BF16)
    if seq > rows:
        a, st_g, st_v = ffn_up_gate(lidx, xn, w["ffn_up"], ffn_prefix, w["ffn_dw_w"], w["ffn_dw_b"],
                                    batch=batch, seq=seq, tm=rows, split=4)
    else:
        u = matmul(lidx, xn, w["ffn_up"], tn=COL)
        a, st_g, st_v = ffn_gate(lidx, u.reshape(batch, seq, 2 * D_FF), ffn_prefix, w["ffn_dw_w"], w["ffn_dw_b"],
                                 rows=rows, nvalid_last=nvalid, prefix_layer=state_layer)
    x = matmul(lidx, a.reshape(m, D_FF), w["ffn_down"], tn=FFN_TN, mode="res", res=x)
    k_rows = h3[:, :, CB_DK * COL:(CB_DK + 1) * COL]
    v_rows = h3[:, :, CB_DV * COL:(CB_DV + 1) * COL]
    ffn_new = jnp.concatenate([st_g, st_v], axis=-1)
    return x, (k_rows, v_rows, ret_new, conv_new, ffn_new)


def kernel(x_prompt, x_sample, cache_k, cache_v, page_table, state_ret, state_conv, state_ffn, norm_mix, w_in, lambda_q1, lambda_k1, lambda_q2, lambda_k2, diff_subln, conv_w, conv_b, conv_ln_g, conv_ln_b, w_branch, w_o, norm_ffn, ffn_up, ffn_dw_w, ffn_dw_b, ffn_down, norm_final):
    bp, lp, _ = x_prompt.shape
    bs, ls, _ = x_sample.shape
    depth = w_in.shape[0]
    n_pool = cache_k.shape[1]
    past_len = page_table.shape[1] * PAGE_SIZE
    sr = SAMPLE_ROWS

    vec3 = lambda a: a.reshape(a.shape[0], 1, a.shape[-1])
    w = dict(norm_mix=vec3(norm_mix), w_in=w_in.astype(BF16), conv_w=conv_w, conv_b=vec3(conv_b),
             conv_ln_g=vec3(conv_ln_g), conv_ln_b=vec3(conv_ln_b), w_branch=w_branch.astype(BF16),
             w_o=w_o.astype(BF16), norm_ffn=vec3(norm_ffn), ffn_up=ffn_up.astype(BF16),
             ffn_dw_w=ffn_dw_w, ffn_dw_b=vec3(ffn_dw_b), ffn_down=ffn_down.astype(BF16))
    lam_vecs = tuple(vec3(a) for a in (lambda_q1, lambda_k1, lambda_q2, lambda_k2))
    subln = vec3(diff_subln)
    lam_inits = jnp.asarray([0.8 - 0.6 * math.exp(-0.3 * l) for l in range(depth)], F32)

    pos_p = jnp.arange(lp, dtype=jnp.int32)
    pos_s = past_len + jnp.arange(sr, dtype=jnp.int32)
    rope_p = _rope_tables(pos_p, RET_D) + _rope_tables(pos_p, DIFF_DH)
    rope_s = tuple(jnp.tile(t, (bs, 1)) for t in _rope_tables(pos_s, RET_D) + _rope_tables(pos_s, DIFF_DH))

    cache_kt = jnp.transpose(cache_k, (0, 1, 3, 4, 2))
    cache_v2 = cache_v.reshape(depth, n_pool, PAGE_SIZE * DIFF_HEADS, DIFF_DV)
    conv_pre_s = jnp.pad(state_conv, ((0, 0), (0, 0), (CONV_HALO - (CONV_WIDTH - 1), 0), (0, 0)))
    ffn_pre_s = jnp.pad(state_ffn, ((0, 0), (0, 0), (FFN_HALO - (FFN_CONV_WIDTH - 1), 0), (0, 0)))
    zero_ret = jnp.zeros((bp, RET_HEADS, RET_D, RET_D), F32)
    zero_conv = jnp.zeros((bp, CONV_HALO, CONV_CH), F32)
    zero_ffn = jnp.zeros((bp, FFN_HALO, 2 * D_FF), F32)

    xp = x_prompt.reshape(bp * lp, D_MODEL)
    xs = jnp.pad(x_sample, ((0, 0), (0, sr - ls), (0, 0))).reshape(bs * sr, D_MODEL)

    outs_p, outs_s = [], []
    for layer in range(depth):
        lidx = jnp.full((1,), layer, jnp.int32)
        xp, st_p = _layer(
            lidx, xp, w, rope_p, lam_inits, batch=bp, seq=lp,
            attend=lambda h3: flash_diff_attention(lidx, h3, lam_vecs, subln, lam_inits, tq=512),
            ret_state=zero_ret, conv_prefix=zero_conv, ffn_prefix=zero_ffn,
            rows=512, ret_chunk=256, nvalid=256, state_layer=False)
        xs, st_s = _layer(
            lidx, xs, w, rope_s, lam_inits, batch=bs, seq=sr,
            attend=lambda h3: decode_diff_attention(lidx, page_table, cache_kt, cache_v2, h3, lam_vecs, subln,
                                                    lam_inits, group=8),
            ret_state=state_ret, conv_prefix=conv_pre_s, ffn_prefix=ffn_pre_s,
            rows=sr, ret_chunk=sr, nvalid=ls, state_layer=True)
        outs_p.append(st_p)
        outs_s.append(st_s)

    one = jnp.zeros((1,), jnp.int32)
    y_prompt = rmsnorm(one, xp, norm_final.reshape(1, 1, D_MODEL), F32).reshape(bp, lp, D_MODEL)
    y_sample = rmsnorm(one, xs, norm_final.reshape(1, 1, D_MODEL), F32).reshape(bs, sr, D_MODEL)[:, :ls]

    def stack(outs, i):
        return jnp.stack([o[i] for o in outs])

    kp = stack(outs_p, 0).reshape(depth, bp, lp, 2 * DIFF_HEADS, DIFF_DH)
    vp = stack(outs_p, 1).reshape(depth, bp, lp, DIFF_HEADS, DIFF_DV)
    rp = stack(outs_p, 2)
    cp = stack(outs_p, 3)[:, :, CONV_HALO - (CONV_WIDTH - 1):]
    fp = stack(outs_p, 4)[:, :, FFN_HALO - (FFN_CONV_WIDTH - 1):]
    ks = stack(outs_s, 0)[:, :, :ls].reshape(depth, bs, ls, 2 * DIFF_HEADS, DIFF_DH)
    vs = stack(outs_s, 1)[:, :, :ls].reshape(depth, bs, ls, DIFF_HEADS, DIFF_DV)
    rs = stack(outs_s, 2)
    cs = stack(outs_s, 3)[:, :, CONV_HALO - (CONV_WIDTH - 1):]
    fs = stack(outs_s, 4)[:, :, FFN_HALO - (FFN_CONV_WIDTH - 1):]
    return (y_prompt, y_sample, kp, vp, rp, cp, fp, ks, vs, rs, cs, fs)
```

```python
import functools
import math

import jax
import jax.numpy as jnp
from jax import lax
from jax.experimental import pallas as pl
from jax.experimental.pallas import tpu as pltpu

F32 = jnp.float32
BF16 = jnp.bfloat16

D_MODEL = 2048
DEPTH = 4
PAGE_SIZE = 128
RET_HEADS = 8
RET_D = 128
DIFF_HEADS = 8
DIFF_DH = 64
DIFF_DV = 128
ROPE_THETA = 10000.0
CONV_CH = 1024
CONV_WIDTH = 31
D_FF = 5632
FFN_CONV_WIDTH = 3
EPS = 1e-6
LANES = 128
SUBLANES = 8
COL = 1024
CB_RQ, CB_RK, CB_RV, CB_RG, CB_DQ, CB_DK, CB_DV, CB_CA, CB_CB, CB_GATE = 0, 1, 2, 3, 4, 5, 6, 7, 8, 9
W_IN_COLS = 15 * COL
NEG = -1e30
VMEM_LIMIT = 56 * 1024 * 1024
SAMPLE_ROWS = 8


def _params(n_grid):
    return pltpu.CompilerParams(dimension_semantics=("arbitrary",) * n_grid,
                                vmem_limit_bytes=VMEM_LIMIT)


def _call(body, *, grid, in_specs, out_specs, out_shape, scratch=(), prefetch, args, name):
    spec = pltpu.PrefetchScalarGridSpec(num_scalar_prefetch=len(prefetch), grid=grid,
                                        in_specs=in_specs, out_specs=out_specs,
                                        scratch_shapes=list(scratch))
    return pl.pallas_call(body, grid_spec=spec, out_shape=out_shape,
                          compiler_params=_params(len(grid)), name=name)(*prefetch, *args)


def _sigmoid(x):
    return 1.0 / (1.0 + jnp.exp(-x))


def _silu(x):
    return x * _sigmoid(x)


def _rmsnorm_body(l_ref, x_ref, g_ref, o_ref):
    x = x_ref[...]
    inv = lax.rsqrt(jnp.mean(x * x, axis=-1, keepdims=True) + EPS)
    o_ref[...] = (x * inv * g_ref[...]).astype(o_ref.dtype)


def rmsnorm(lidx, x, g, out_dtype):
    m, d = x.shape
    tm = min(m, 512)
    return _call(
        _rmsnorm_body, grid=(m // tm,),
        in_specs=[pl.BlockSpec((tm, d), lambda i, l: (i, 0)),
                  pl.BlockSpec((None, 1, d), lambda i, l: (l[0], 0, 0))],
        out_specs=pl.BlockSpec((tm, d), lambda i, l: (i, 0)),
        out_shape=jax.ShapeDtypeStruct((m, d), out_dtype),
        prefetch=(lidx,), args=(x, g), name="rmsnorm")


def _rope_cols(x, cos, sin, head_dim):
    if head_dim == LANES:
        rot = pltpu.roll(x, LANES // 2, 1)
    else:
        half = head_dim // 2
        lane = lax.broadcasted_iota(jnp.int32, x.shape, 1)
        rot = jnp.where((lane % head_dim) < half, pltpu.roll(x, LANES - half, 1), pltpu.roll(x, half, 1))
    return x * cos + rot * sin


def _mm_body(l_ref, x_ref, w_ref, *rest, mode):
    acc = jnp.dot(x_ref[...], w_ref[...], preferred_element_type=F32)
    if mode == "plain":
        (o_ref,) = rest
        o_ref[...] = acc.astype(o_ref.dtype)
    elif mode == "res":
        r_ref, o_ref = rest
        o_ref[...] = r_ref[...] + acc
    else:
        c128, s128, c64, s64, o_ref = rest
        j = pl.program_id(0)
        is128 = jnp.logical_or(j == CB_RQ, j == CB_RK)
        is64 = jnp.logical_or(j == CB_DQ, j == CB_DK)

        @pl.when(is128)
        def _():
            for g in range(COL // LANES):
                cs = slice(g * LANES, (g + 1) * LANES)
                o_ref[:, cs] = _rope_cols(acc[:, cs], c128[...], s128[...], RET_D)

        @pl.when(is64)
        def _():
            for g in range(COL // LANES):
                cs = slice(g * LANES, (g + 1) * LANES)
                o_ref[:, cs] = _rope_cols(acc[:, cs], c64[...], s64[...], DIFF_DH)

        @pl.when(jnp.logical_not(jnp.logical_or(is128, is64)))
        def _():
            o_ref[...] = acc


def matmul(lidx, x, w, *, tn, tm=512, mode="plain", res=None, rope=None, out_dtype=F32):
    m, k = x.shape
    n = w.shape[-1]
    tm = min(m, tm)
    grid = (n // tn, m // tm)
    in_specs = [pl.BlockSpec((tm, k), lambda j, i, l: (i, 0)),
                pl.BlockSpec((None, k, tn), lambda j, i, l: (l[0], 0, j))]
    args = [x, w]
    if mode == "res":
        in_specs.append(pl.BlockSpec((tm, tn), lambda j, i, l: (i, j)))
        args.append(res)
    elif mode == "rope":
        assert tn == COL
        nt = rope[0].shape[0] // tm
        for t in rope:
            in_specs.append(pl.BlockSpec((tm, LANES), lambda j, i, l: (i % nt, 0)))
            args.append(t)
    return _call(
        functools.partial(_mm_body, mode=mode), grid=grid, in_specs=in_specs,
        out_specs=pl.BlockSpec((tm, tn), lambda j, i, l: (i, j)),
        out_shape=jax.ShapeDtypeStruct((m, n), out_dtype),
        prefetch=(lidx,), args=args, name="mm_" + mode)


def _ret_body(l_ref, q_ref, k_ref, v_ref, g_ref, s0_ref, d_ref, xi_ref, zeta_ref, gc_ref, *rest, chunk, rope):
    if rope:
        cos_ref, sin_ref, o_ref, sout_ref, s_scr = rest
    else:
        o_ref, sout_ref, s_scr = rest
    t = pl.program_id(1)

    @pl.when(t == 0)
    def _():
        s_scr[...] = s0_ref[...]

    rows = q_ref.shape[0]
    cp = d_ref.shape[-1]
    nt_dims = (((1,), (1,)), ((), ()))
    tn_dims = (((0,), (0,)), ((), ()))
    for h in range(RET_HEADS):
        cs = slice(h * RET_D, (h + 1) * RET_D)
        for c in range(rows // chunk):
            rs = slice(c * chunk, (c + 1) * chunk)
            q = q_ref[rs, cs]
            k = k_ref[rs, cs]
            if rope:
                q = _rope_cols(q, cos_ref[rs], sin_ref[rs], RET_D)
                k = _rope_cols(k, cos_ref[rs], sin_ref[rs], RET_D)
            k = k * (RET_D ** -0.5)
            v = v_ref[rs, cs]
            if cp > chunk:
                pad = jnp.zeros((cp - chunk, RET_D), F32)
                q, k, v = (jnp.concatenate([a, pad], axis=0) for a in (q, k, v))
            qb, kb, vb = q.astype(BF16), k.astype(BF16), v.astype(BF16)
            inner = lax.dot_general(qb, kb, nt_dims, preferred_element_type=F32) * d_ref[h]
            s = s_scr[h]
            o = (jnp.dot(inner.astype(BF16), vb, preferred_element_type=F32)
                 + jnp.dot(qb, s.astype(BF16), preferred_element_type=F32) * xi_ref[h])
            kz = (k * zeta_ref[h]).astype(BF16)
            s_scr[h] = s * gc_ref[h] + lax.dot_general(kz, vb, tn_dims, preferred_element_type=F32)
            o = o[:chunk]
            o = o * lax.rsqrt(jnp.mean(o * o, axis=-1, keepdims=True) + EPS)
            o_ref[rs, cs] = (o * _silu(g_ref[rs, cs])).astype(o_ref.dtype)

    @pl.when(t == pl.num_programs(1) - 1)
    def _():
        sout_ref[...] = s_scr[...]


def _ret_consts(cp, nvalid):
    log_g = jnp.log1p(-jnp.exp2(-5.0 - jnp.arange(RET_HEADS, dtype=F32)))
    idx = jnp.arange(cp, dtype=F32)
    dist = idx[:, None] - idx[None, :]
    d = jnp.where(dist >= 0, jnp.exp(log_g[:, None, None] * jnp.maximum(dist, 0.0)), 0.0)
    xi = jnp.exp(log_g[:, None] * (idx + 1.0))[:, :, None]
    zeta = jnp.where(idx < nvalid, jnp.exp(log_g[:, None] * (nvalid - 1.0 - idx)), 0.0)[:, :, None]
    gc = jnp.exp(log_g * nvalid)[:, None, None]
    return d.astype(F32), xi.astype(F32), zeta.astype(F32), gc.astype(F32)


def retention(lidx, h3, s0, *, rows, chunk, nvalid, s0_layer, rope=None):
    b, seq, _ = h3.shape
    cp = max(chunk, LANES)
    d, xi, zeta, gc = _ret_consts(cp, nvalid)

    def hspec(cb):
        return pl.BlockSpec((None, rows, COL), lambda bi, t, l: (bi, t, cb))

    if s0_layer:
        s0_spec = pl.BlockSpec((None, None, RET_HEADS, RET_D, RET_D), lambda bi, t, l: (l[0], bi, 0, 0, 0))
    else:
        s0_spec = pl.BlockSpec((None, RET_HEADS, RET_D, RET_D), lambda bi, t, l: (bi, 0, 0, 0))
    full3 = lambda a: pl.BlockSpec(a.shape, lambda bi, t, l: (0, 0, 0))
    tabs = list(rope) if rope else []
    return _call(
        functools.partial(_ret_body, chunk=chunk, rope=bool(rope)), grid=(b, seq // rows),
        in_specs=[hspec(CB_RQ), hspec(CB_RK), hspec(CB_RV), hspec(CB_RG), s0_spec,
                  full3(d), full3(xi), full3(zeta), full3(gc)]
        + [pl.BlockSpec((rows, LANES), lambda bi, t, l: (t, 0)) for _ in tabs],
        out_specs=[pl.BlockSpec((None, rows, COL), lambda bi, t, l: (bi, t, 0)),
                   pl.BlockSpec((None, RET_HEADS, RET_D, RET_D), lambda bi, t, l: (bi, 0, 0, 0))],
        out_shape=[jax.ShapeDtypeStruct((b, seq, COL), BF16),
                   jax.ShapeDtypeStruct((b, RET_HEADS, RET_D, RET_D), F32)],
        scratch=[pltpu.VMEM((RET_HEADS, RET_D, RET_D), F32)],
        prefetch=(lidx,), args=(h3, h3, h3, h3, s0, d, xi, zeta, gc, *tabs), name="retention")


def _lambda(lq1, lk1, lq2, lk2, lam_init):
    a = jnp.exp(jnp.sum(lq1[...] * lk1[...], axis=-1, keepdims=True))
    b = jnp.exp(jnp.sum(lq2[...] * lk2[...], axis=-1, keepdims=True))
    return a - b + lam_init


def _diff_out(o, sub_ref, lam_init):
    o = o * lax.rsqrt(jnp.mean(o * o, axis=-1, keepdims=True) + EPS)
    return o * sub_ref[...] * (1.0 - lam_init)


def _softmax_step(s, vb, m, l, acc):
    m_new = jnp.maximum(m, jnp.max(s, axis=-1, keepdims=True))
    p = jnp.exp(s - m_new)
    alpha = jnp.exp(m - m_new)
    l = alpha * l + jnp.sum(p, axis=-1, keepdims=True)
    acc = alpha * acc + jnp.dot(p.astype(BF16), vb, preferred_element_type=F32)
    return m_new, l, acc


FLASH_RG = 16
LOG2E = math.log2(math.e)


def _flash_body(l_ref, q_ref, k_ref, v_ref, cq_ref, sq_ref, ck_ref, sk_ref, lq1, lk1, lq2, lk2, sub_ref, li_ref,
                o_ref, kr_ref, kb_scr, vb_scr, q2_scr, s_scr, p_scr, m_scr, a_scr, acc_scr, *, tq):
    qi = pl.program_id(2)
    nt_dims = (((1,), (1,)), ((), ()))

    @pl.when(qi == 0)
    def _():
        kr = _rope_cols(k_ref[...], ck_ref[...], sk_ref[...], DIFF_DH)
        kr_ref[...] = kr
        kb_scr[...] = kr.astype(BF16)
        vb_scr[:, 0:DIFF_DV] = v_ref[...].astype(BF16)
        vb_scr[:, DIFF_DV:2 * DIFF_DV] = jnp.ones((vb_scr.shape[0], DIFF_DV), BF16)

    q = _rope_cols(q_ref[...], cq_ref[...], sq_ref[...], DIFF_DH) * (DIFF_DH ** -0.5 * LOG2E)
    lane = lax.broadcasted_iota(jnp.int32, q.shape, 1)
    q2_scr[0:tq] = jnp.where(lane < DIFF_DH, q, 0.0).astype(BF16)
    q2_scr[tq:2 * tq] = jnp.where(lane >= DIFF_DH, q, 0.0).astype(BF16)
    m_scr[...] = jnp.full(m_scr.shape, NEG, F32)
    acc_scr[...] = jnp.zeros(acc_scr.shape, F32)

    def scores(j, slot):
        off = pl.multiple_of(j * tq, tq)
        s_scr[slot] = lax.dot_general(q2_scr[...], kb_scr[pl.ds(off, tq), :], nt_dims,
                                      preferred_element_type=F32)

    def softmax_pv(j, slot, masked):
        off = pl.multiple_of(j * tq, tq)
        for r0 in range(0, 2 * tq, FLASH_RG):
            rs = slice(r0, r0 + FLASH_RG)
            s = s_scr[slot, rs, :]
            if masked:
                row = lax.broadcasted_iota(jnp.int32, s.shape, 0) + (r0 % tq)
                col = lax.broadcasted_iota(jnp.int32, s.shape, 1)
                s = jnp.where(col <= row, s, NEG)
            m_old = m_scr[rs]
            m_new = jnp.maximum(m_old, jnp.broadcast_to(jnp.max(s, axis=-1, keepdims=True), m_old.shape))
            p = jnp.exp2(s - pltpu.repeat(m_new, tq // LANES, axis=1))
            a_scr[rs] = jnp.exp2(m_old - m_new)
            m_scr[rs] = m_new
            p_scr[rs, :] = p.astype(BF16)
        vb = vb_scr[pl.ds(off, tq), :]
        for r0 in range(0, 2 * tq, tq):
            rs = slice(r0, r0 + tq)
            acc_scr[rs] = (pltpu.repeat(a_scr[rs], 2, axis=1) * acc_scr[rs]
                           + jnp.dot(p_scr[rs, :], vb, preferred_element_type=F32))

    scores(0, 0)

    def pair(jj, carry):
        j = 2 * jj
        scores(j + 1, 1)
        softmax_pv(j, 0, False)
        scores(j + 2, 0)
        softmax_pv(j + 1, 1, False)
        return carry

    lax.fori_loop(0, qi // 2, pair, 0)

    @pl.when(qi % 2 == 0)
    def _():
        softmax_pv(qi, 0, True)

    @pl.when(qi % 2 == 1)
    def _():
        scores(qi, 1)
        softmax_pv(qi - 1, 0, False)
        softmax_pv(qi, 1, True)

    lam_init = li_ref[l_ref[0]]
    lam = _lambda(lq1, lk1, lq2, lk2, lam_init)
    on = acc_scr[:, 0:DIFF_DV] / acc_scr[:, DIFF_DV:2 * DIFF_DV]
    o = on[:tq] - lam * on[tq:]
    o_ref[...] = _diff_out(o, sub_ref, lam_init).astype(o_ref.dtype)


def _lam_specs(nidx):
    def vec(width):
        if nidx == 3:
            return pl.BlockSpec((None, 1, width), lambda a, b, c, l: (l[0], 0, 0))
        return pl.BlockSpec((None, 1, width), lambda a, b, l, pt: (l[0], 0, 0))
    return [vec(DIFF_DH)] * 4 + [vec(DIFF_DV), pl.BlockSpec(memory_space=pltpu.SMEM)]


def flash_diff_attention(lidx, h3, rope, lam_vecs, subln, lam_inits, *, tq):
    b, seq, _ = h3.shape
    per = COL // LANES
    cos, sin = rope

    def kvspec(cb):
        return pl.BlockSpec((None, seq, LANES), lambda bi, p, qi, l: (bi, 0, cb * per + p))

    q_tab = pl.BlockSpec((tq, LANES), lambda bi, p, qi, l: (qi, 0))
    k_tab = pl.BlockSpec((seq, LANES), lambda bi, p, qi, l: (0, 0))
    return _call(
        functools.partial(_flash_body, tq=tq), grid=(b, DIFF_HEADS, seq // tq),
        in_specs=[pl.BlockSpec((None, tq, LANES), lambda bi, p, qi, l: (bi, qi, CB_DQ * per + p)),
                  kvspec(CB_DK), kvspec(CB_DV), q_tab, q_tab, k_tab, k_tab] + _lam_specs(3),
        out_specs=[pl.BlockSpec((None, tq, LANES), lambda bi, p, qi, l: (bi, qi, p)),
                   pl.BlockSpec((None, seq, LANES), lambda bi, p, qi, l: (bi, 0, p))],
        out_shape=[jax.ShapeDtypeStruct((b, seq, COL), BF16), jax.ShapeDtypeStruct((b, seq, COL), F32)],
        scratch=[pltpu.VMEM((seq, LANES), BF16), pltpu.VMEM((seq, 2 * DIFF_DV), BF16),
                 pltpu.VMEM((2 * tq, LANES), BF16), pltpu.VMEM((2, 2 * tq, tq), F32),
                 pltpu.VMEM((2 * tq, tq), BF16), pltpu.VMEM((2 * tq, LANES), F32),
                 pltpu.VMEM((2 * tq, LANES), F32), pltpu.VMEM((2 * tq, 2 * DIFF_DV), F32)],
        prefetch=(lidx,), args=(h3, h3, h3, cos, sin, cos, sin, *lam_vecs, subln, lam_inits), name="flash_diff")


def _decode_body(l_ref, pt_ref, *refs, group):
    k_pages = refs[:group]
    v_pages = refs[group:2 * group]
    (q_ref, kn_ref, vn_ref, lq1, lk1, lq2, lk2, sub_ref, li_ref,
     o_ref, qbd_scr, m_scr, l_scr, acc_scr) = refs[2 * group:]
    s_id = pl.program_id(1)
    pair_rows = 2 * SAMPLE_ROWS
    nt_dims = (((1,), (1,)), ((), ()))

    @pl.when(s_id == 0)
    def _():
        q = q_ref[...] * (DIFF_DH ** -0.5)
        qt = jnp.concatenate([q] * (2 * DIFF_HEADS), axis=0)
        rh = lax.broadcasted_iota(jnp.int32, qt.shape, 0) // SAMPLE_ROWS
        ch = lax.broadcasted_iota(jnp.int32, qt.shape, 1) // DIFF_DH
        qbd_scr[...] = jnp.where(rh == ch, qt, 0.0).astype(BF16)
        m_scr[...] = jnp.full(m_scr.shape, NEG, F32)
        l_scr[...] = jnp.zeros(l_scr.shape, F32)
        acc_scr[...] = jnp.zeros(acc_scr.shape, F32)

    qbd = qbd_scr[...]
    s = jnp.concatenate(
        [jnp.dot(qbd, kp[...].reshape(COL, PAGE_SIZE).astype(BF16), preferred_element_type=F32)
         for kp in k_pages], axis=1)
    m = m_scr[...]
    m_new = jnp.maximum(m, jnp.max(s, axis=-1, keepdims=True))
    p = jnp.exp(s - m_new).astype(BF16)
    alpha = jnp.exp(m - m_new)
    l_scr[...] = alpha * l_scr[...] + jnp.sum(p.astype(F32), axis=-1, keepdims=True)
    pv = []
    for hp in range(DIFF_HEADS):
        vh = jnp.concatenate([vp[pl.ds(hp, PAGE_SIZE, stride=DIFF_HEADS), :] for vp in v_pages],
                             axis=0).astype(BF16)
        pv.append(jnp.dot(p[hp * pair_rows:(hp + 1) * pair_rows], vh, preferred_element_type=F32))
    acc_scr[...] = alpha * acc_scr[...] + jnp.concatenate(pv, axis=0)
    m_scr[...] = m_new

    @pl.when(s_id == pl.num_programs(1) - 1)
    def _():
        pad = jnp.zeros((LANES - SAMPLE_ROWS, COL), F32)
        kn = jnp.concatenate([kn_ref[...], pad], axis=0).astype(BF16)
        vn = jnp.concatenate([vn_ref[...], pad], axis=0).astype(BF16)
        sn = lax.dot_general(qbd, kn, nt_dims, preferred_element_type=F32)
        tok = lax.broadcasted_iota(jnp.int32, sn.shape, 0) % SAMPLE_ROWS
        key = lax.broadcasted_iota(jnp.int32, sn.shape, 1)
        sn = jnp.where(key <= tok, sn, NEG)
        mo = m_scr[...]
        mf = jnp.maximum(mo, jnp.max(sn, axis=-1, keepdims=True))
        pn = jnp.exp(sn - mf).astype(BF16)
        af = jnp.exp(mo - mf)
        lf = af * l_scr[...] + jnp.sum(pn.astype(F32), axis=-1, keepdims=True)
        pvn = jnp.dot(pn, vn, preferred_element_type=F32)
        pvn = jnp.concatenate([pvn[hp * pair_rows:(hp + 1) * pair_rows, hp * DIFF_DV:(hp + 1) * DIFF_DV]
                               for hp in range(DIFF_HEADS)], axis=0)
        on = (af * acc_scr[...] + pvn) / lf
        lam_init = li_ref[l_ref[0]]
        lam = _lambda(lq1, lk1, lq2, lk2, lam_init)
        for hp in range(DIFF_HEADS):
            r0 = hp * pair_rows
            o = on[r0:r0 + SAMPLE_ROWS] - lam * on[r0 + SAMPLE_ROWS:r0 + pair_rows]
            o_ref[:, hp * DIFF_DV:(hp + 1) * DIFF_DV] = _diff_out(o, sub_ref, lam_init).astype(o_ref.dtype)


def decode_diff_attention(lidx, page_table, cache_kt, cache_v, h3, lam_vecs, subln, lam_inits, *, group):
    b = h3.shape[0]
    n_pages = page_table.shape[1]
    rows = 2 * DIFF_HEADS * SAMPLE_ROWS

    def page_spec(shape, g):
        zeros = (0,) * len(shape)
        return pl.BlockSpec((None, None) + shape, lambda bi, s, l, pt: (l[0], pt[bi, s * group + g]) + zeros)

    def hspec(cb):
        return pl.BlockSpec((None, SAMPLE_ROWS, COL), lambda bi, s, l, pt: (bi, 0, cb))

    k_specs = [page_spec((2 * DIFF_HEADS, DIFF_DH, PAGE_SIZE), g) for g in range(group)]
    v_specs = [page_spec((PAGE_SIZE * DIFF_HEADS, DIFF_DV), g) for g in range(group)]
    return _call(
        functools.partial(_decode_body, group=group), grid=(b, n_pages // group),
        in_specs=k_specs + v_specs + [hspec(CB_DQ), hspec(CB_DK), hspec(CB_DV)] + _lam_specs(2),
        out_specs=pl.BlockSpec((None, SAMPLE_ROWS, COL), lambda bi, s, l, pt: (bi, 0, 0)),
        out_shape=jax.ShapeDtypeStruct((b, SAMPLE_ROWS, COL), BF16),
        scratch=[pltpu.VMEM((rows, COL), BF16), pltpu.VMEM((rows, 1), F32), pltpu.VMEM((rows, 1), F32),
                 pltpu.VMEM((rows, DIFF_DV), F32)],
        prefetch=(lidx, page_table),
        args=([cache_kt] * group + [cache_v] * group + [h3, h3, h3] + list(lam_vecs) + [subln, lam_inits]),
        name="decode_diff")


CONV_HALO = 32
CONV_RC = 32
CONV_CW = 512


def _conv_body(l_ref, ca_ref, cb_ref, pre_ref, w_ref, b_ref, lg_ref, lb_ref, o_ref, st_ref, buf, y_scr,
               *, nvalid_last):
    t = pl.program_id(1)
    rows = ca_ref.shape[0]

    @pl.when(t == 0)
    def _():
        buf[0:CONV_HALO] = pre_ref[...]

    if rows >= CONV_HALO:
        @pl.when(t > 0)
        def _():
            buf[0:CONV_HALO] = buf[rows:rows + CONV_HALO]

    buf[CONV_HALO:CONV_HALO + rows] = ca_ref[...] * _sigmoid(cb_ref[...])
    base = CONV_HALO - (CONV_WIDTH - 1)
    rc = min(rows, CONV_RC)
    for c0 in range(0, CONV_CH, CONV_CW):
        cs = slice(c0, c0 + CONV_CW)
        for r0 in range(0, rows, rc):
            acc = jnp.zeros((rc, CONV_CW), F32) + b_ref[:, cs]
            for ph in range(SUBLANES):
                taps = [j for j in range(CONV_WIDTH) if (base + j) % SUBLANES == ph]
                n = rc + (SUBLANES if ph else 0)
                part = None
                for j in taps:
                    start = r0 + base + j - ph
                    term = buf[start:start + n, cs] * w_ref[j:j + 1, cs]
                    part = term if part is None else part + term
                acc = acc + part[ph:ph + rc]
            y_scr[r0:r0 + rc, cs] = acc
    y = y_scr[...]
    mu = jnp.mean(y, axis=-1, keepdims=True)
    yc = y - mu
    var = jnp.mean(yc * yc, axis=-1, keepdims=True)
    y = yc * lax.rsqrt(var + EPS) * lg_ref[...] + lb_ref[...]
    o_ref[...] = _silu(y).astype(o_ref.dtype)

    @pl.when(t == pl.num_programs(1) - 1)
    def _():
        st_ref[...] = buf[nvalid_last:nvalid_last + CONV_HALO]


def conv_module(lidx, h3, prefix, conv_w, conv_b, ln_g, ln_b, *, rows, nvalid_last, prefix_layer):
    b, seq, _ = h3.shape
    if prefix_layer:
        pre_spec = pl.BlockSpec((None, None, CONV_HALO, CONV_CH), lambda bi, t, l: (l[0], bi, 0, 0))
    else:
        pre_spec = pl.BlockSpec((None, CONV_HALO, CONV_CH), lambda bi, t, l: (bi, 0, 0))
    vec = pl.BlockSpec((None, 1, CONV_CH), lambda bi, t, l: (l[0], 0, 0))
    return _call(
        functools.partial(_conv_body, nvalid_last=nvalid_last), grid=(b, seq // rows),
        in_specs=[pl.BlockSpec((None, rows, COL), lambda bi, t, l: (bi, t, CB_CA)),
                  pl.BlockSpec((None, rows, COL), lambda bi, t, l: (bi, t, CB_CB)),
                  pre_spec,
                  pl.BlockSpec((None, CONV_WIDTH, CONV_CH), lambda bi, t, l: (l[0], 0, 0)),
                  vec, vec, vec],
        out_specs=[pl.BlockSpec((None, rows, CONV_CH), lambda bi, t, l: (bi, t, 0)),
                   pl.BlockSpec((None, CONV_HALO, CONV_CH), lambda bi, t, l: (bi, 0, 0))],
        out_shape=[jax.ShapeDtypeStruct((b, seq, CONV_CH), BF16),
                   jax.ShapeDtypeStruct((b, CONV_HALO, CONV_CH), F32)],
        scratch=[pltpu.VMEM((CONV_HALO + rows, CONV_CH), F32), pltpu.VMEM((rows, CONV_CH), F32)],
        prefetch=(lidx,), args=(h3, h3, prefix, conv_w, conv_b, ln_g, ln_b), name="conv_module")


def _merge_body(l_ref, ro_ref, do_ref, co_ref, w0, w1, w2, g0, g1, g2, o_ref):
    acc = _sigmoid(g0[...]) * jnp.dot(ro_ref[...], w0[...], preferred_element_type=F32)
    acc = acc + _sigmoid(g1[...]) * jnp.dot(do_ref[...], w1[...], preferred_element_type=F32)
    acc = acc + _sigmoid(g2[...]) * jnp.dot(co_ref[...], w2[...], preferred_element_type=F32)
    o_ref[...] = acc.astype(o_ref.dtype)


def merge(lidx, ro, do, co, h, w_branch):
    m = ro.shape[0]
    tm = min(m, 512)
    tn = COL
    nb = D_MODEL // tn
    act = pl.BlockSpec((tm, COL), lambda j, i, l: (i, 0))

    def wspec(r):
        return pl.BlockSpec((None, COL, tn), lambda j, i, l: (l[0], r, j))

    def gspec(r):
        return pl.BlockSpec((tm, tn), lambda j, i, l: (i, CB_GATE + r * nb + j))

    return _call(
        _merge_body, grid=(nb, m // tm),
        in_specs=[act, act, act, wspec(0), wspec(1), wspec(2), gspec(0), gspec(1), gspec(2)],
        out_specs=pl.BlockSpec((tm, tn), lambda j, i, l: (i, j)),
        out_shape=jax.ShapeDtypeStruct((m, D_MODEL), BF16),
        prefetch=(lidx,), args=(ro, do, co, w_branch, w_branch, w_branch, h, h, h), name="merge")


FFN_TN = 512
FFN_HALO = 8


def _ffn_gate_body(l_ref, ug_ref, uv_ref, pg_ref, pv_ref, wg_ref, wv_ref, bg_ref, bv_ref,
                   o_ref, sg_ref, sv_ref, buf, *, nvalid_last):
    t = pl.program_id(2)
    rows = ug_ref.shape[0]
    ys = []
    for half, (u_ref, p_ref, w_ref, b_ref, s_ref) in enumerate(
            ((ug_ref, pg_ref, wg_ref, bg_ref, sg_ref), (uv_ref, pv_ref, wv_ref, bv_ref, sv_ref))):
        @pl.when(t == 0)
        def _():
            buf[half, 0:FFN_HALO] = p_ref[...]

        if rows >= FFN_HALO:
            @pl.when(t > 0)
            def _():
                buf[half, 0:FFN_HALO] = buf[half, rows:rows + FFN_HALO]

        buf[half, FFN_HALO:FFN_HALO + rows] = u_ref[...]
        base = FFN_HALO - (FFN_CONV_WIDTH - 1)
        y = b_ref[...] + buf[half, base:base + rows] * w_ref[0:1]
        for j in range(1, FFN_CONV_WIDTH):
            y = y + buf[half, base + j:base + j + rows] * w_ref[j:j + 1]
        ys.append(y)

        @pl.when(t == pl.num_programs(2) - 1)
        def _():
            s_ref[...] = buf[half, nvalid_last:nvalid_last + FFN_HALO]

    o_ref[...] = (_silu(ys[0]) * ys[1]).astype(o_ref.dtype)


def ffn_gate(lidx, u3, prefix, dw_w, dw_b, *, rows, nvalid_last, prefix_layer, tn=FFN_TN):
    b, seq, _ = u3.shape
    nc = D_FF // tn

    def pspec(off):
        if prefix_layer:
            return pl.BlockSpec((None, None, FFN_HALO, tn), lambda bi, c, t, l: (l[0], bi, 0, c + off))
        return pl.BlockSpec((None, FFN_HALO, tn), lambda bi, c, t, l: (bi, 0, c + off))

    def uspec(off):
        return pl.BlockSpec((None, rows, tn), lambda bi, c, t, l: (bi, t, c + off))

    def wspec(off):
        return pl.BlockSpec((None, FFN_CONV_WIDTH, tn), lambda bi, c, t, l: (l[0], 0, c + off))

    def bspec(off):
        return pl.BlockSpec((None, 1, tn), lambda bi, c, t, l: (l[0], 0, c + off))

    st_spec = pl.BlockSpec((None, FFN_HALO, tn), lambda bi, c, t, l: (bi, 0, c))
    st_shape = jax.ShapeDtypeStruct((b, FFN_HALO, D_FF), F32)
    return _call(
        functools.partial(_ffn_gate_body, nvalid_last=nvalid_last), grid=(b, nc, seq // rows),
        in_specs=[uspec(0), uspec(nc), pspec(0), pspec(nc), wspec(0), wspec(nc), bspec(0), bspec(nc)],
        out_specs=[pl.BlockSpec((None, rows, tn), lambda bi, c, t, l: (bi, t, c)), st_spec, st_spec],
        out_shape=[jax.ShapeDtypeStruct((b, seq, D_FF), BF16), st_shape, st_shape],
        scratch=[pltpu.VMEM((2, FFN_HALO + rows, tn), F32)],
        prefetch=(lidx,), args=(u3, u3, prefix, prefix, dw_w, dw_w, dw_b, dw_b), name="ffn_gate")


def _up_gate_body(l_ref, x_ref, wg_ref, wv_ref, pg_ref, pv_ref, cg_ref, cv_ref, bg_ref, bv_ref,
                  o_ref, sg_ref, sv_ref, buf, *, nt, split):
    t = pl.program_id(1) % nt
    tm = x_ref.shape[0]
    hm = tm // split
    base = FFN_HALO - (FFN_CONV_WIDTH - 1)
    halves = ((wg_ref, pg_ref, cg_ref, bg_ref, sg_ref), (wv_ref, pv_ref, cv_ref, bv_ref, sv_ref))

    @pl.when(jnp.logical_and(pl.program_id(0) == 0, pl.program_id(1) == 0))
    def _():
        buf[...] = jnp.zeros(buf.shape, F32)

    tails = [jnp.where(t == 0, p_ref[...], buf[half]) for half, (_, p_ref, _, _, _) in enumerate(halves)]
    for r in range(split):
        ys = []
        for half, (w_ref, p_ref, c_ref, b_ref, s_ref) in enumerate(halves):
            u = jnp.dot(x_ref[r * hm:(r + 1) * hm, :], w_ref[...], preferred_element_type=F32)
            ext = jnp.concatenate([tails[half], u], axis=0)
            y = b_ref[...] + u * c_ref[FFN_CONV_WIDTH - 1:FFN_CONV_WIDTH]
            for j in range(FFN_CONV_WIDTH - 1):
                y = y + ext[base + j:base + j + hm] * c_ref[j:j + 1]
            ys.append(y)
            tails[half] = u[hm - FFN_HALO:]
        o_ref[r * hm:(r + 1) * hm, :] = (_silu(ys[0]) * ys[1]).astype(o_ref.dtype)
    for half, (w_ref, p_ref, c_ref, b_ref, s_ref) in enumerate(halves):
        buf[half] = tails[half]
        s_ref[...] = tails[half]


def ffn_up_gate(lidx, xn, w_up, prefix, dw_w, dw_b, *, batch, seq, tm, split):
    m, k = xn.shape
    nc = D_FF // FFN_TN
    nt = seq // tm

    def wspec(off):
        return pl.BlockSpec((None, k, FFN_TN), lambda c, i, l: (l[0], 0, c + off))

    def pspec(off):
        return pl.BlockSpec((None, FFN_HALO, FFN_TN), lambda c, i, l: (i // nt, 0, c + off))

    def cspec(off):
        return pl.BlockSpec((None, FFN_CONV_WIDTH, FFN_TN), lambda c, i, l: (l[0], 0, c + off))

    def bspec(off):
        return pl.BlockSpec((None, 1, FFN_TN), lambda c, i, l: (l[0], 0, c + off))

    st_spec = pl.BlockSpec((None, FFN_HALO, FFN_TN), lambda c, i, l: (i // nt, 0, c))
    st_shape = jax.ShapeDtypeStruct((batch, FFN_HALO, D_FF), F32)
    return _call(
        functools.partial(_up_gate_body, nt=nt, split=split), grid=(nc, m // tm),
        in_specs=[pl.BlockSpec((tm, k), lambda c, i, l: (i, 0)), wspec(0), wspec(nc), pspec(0), pspec(nc),
                  cspec(0), cspec(nc), bspec(0), bspec(nc)],
        out_specs=[pl.BlockSpec((tm, FFN_TN), lambda c, i, l: (i, c)), st_spec, st_spec],
        out_shape=[jax.ShapeDtypeStruct((m, D_FF), BF16), st_shape, st_shape],
        scratch=[pltpu.VMEM((2, FFN_HALO, FFN_TN), F32)],
        prefetch=(lidx,), args=(xn, w_up, w_up, prefix, prefix, dw_w, dw_w, dw_b, dw_b), name="ffn_up_gate")


def _rope_tables(pos, head_dim):
    half = head_dim // 2
    inv = ROPE_THETA ** (-jnp.arange(half, dtype=F32) * 2.0 / head_dim)
    ang = pos.astype(F32)[:, None] * inv[None, :]
    cos, sin = jnp.cos(ang), jnp.sin(ang)
    reps = LANES // head_dim
    return (jnp.tile(jnp.concatenate([cos, cos], axis=-1), (1, reps)),
            jnp.tile(jnp.concatenate([-sin, sin], axis=-1), (1, reps)))


def _layer(lidx, x, w, rope, lam_inits, *, batch, seq, attend, ret_state, conv_prefix, ffn_prefix,
           rows, mm_rows, ret_chunk, nvalid, state_layer, rope_in_mm):
    m = batch * seq
    xn = rmsnorm(lidx, x, w["norm_mix"], BF16)
    if rope_in_mm:
        h = matmul(lidx, xn, w["w_in"], tn=COL, tm=mm_rows, mode="rope", rope=rope)
    else:
        h = matmul(lidx, xn, w["w_in"], tn=COL, tm=mm_rows)
    h3 = h.reshape(batch, seq, W_IN_COLS)
    ro, ret_new = retention(lidx, h3, ret_state, rows=rows, chunk=ret_chunk, nvalid=nvalid, s0_layer=state_layer,
                            rope=None if rope_in_mm else rope[:2])
    do, k_rows = attend(h3)
    co, conv_new = conv_module(lidx, h3, conv_prefix, w["conv_w"], w["conv_b"], w["conv_ln_g"], w["conv_ln_b"],
                               rows=rows, nvalid_last=nvalid if seq == rows else rows,
                               prefix_layer=state_layer)
    mg = merge(lidx, ro.reshape(m, COL), do.reshape(m, COL), co.reshape(m, COL), h, w["w_branch"])
    x = matmul(lidx, mg, w["w_o"], tn=COL, tm=mm_rows, mode="res", res=x)
    xn = rmsnorm(lidx, x, w["norm_ffn"], BF16)
    if seq > rows:
        a, st_g, st_v = ffn_up_gate(lidx, xn, w["ffn_up"], ffn_prefix, w["ffn_dw_w"], w["ffn_dw_b"],
                                    batch=batch, seq=seq, tm=mm_rows, split=4)
    else:
        u = matmul(lidx, xn, w["ffn_up"], tn=COL)
        a, st_g, st_v = ffn_gate(lidx, u.reshape(batch, seq, 2 * D_FF), ffn_prefix, w["ffn_dw_w"], w["ffn_dw_b"],
                                 rows=rows, nvalid_last=nvalid, prefix_layer=state_layer, tn=D_FF)
    x = matmul(lidx, a.reshape(m, D_FF), w["ffn_down"], tn=FFN_TN, tm=mm_rows, mode="res", res=x)
    if k_rows is None:
        k_rows = h3[:, :, CB_DK * COL:(CB_DK + 1) * COL]
    v_rows = h3[:, :, CB_DV * COL:(CB_DV + 1) * COL]
    ffn_new = jnp.concatenate([st_g, st_v], axis=-1)
    return x, (k_rows, v_rows, ret_new, conv_new, ffn_new)


def kernel(x_prompt, x_sample, cache_k, cache_v, page_table, state_ret, state_conv, state_ffn, norm_mix, w_in, lambda_q1, lambda_k1, lambda_q2, lambda_k2, diff_subln, conv_w, conv_b, conv_ln_g, conv_ln_b, w_branch, w_o, norm_ffn, ffn_up, ffn_dw_w, ffn_dw_b, ffn_down, norm_final):
    bp, lp, _ = x_prompt.shape
    bs, ls, _ = x_sample.shape
    depth = w_in.shape[0]
    n_pool = cache_k.shape[1]
    past_len = page_table.shape[1] * PAGE_SIZE
    sr = SAMPLE_ROWS

    vec3 = lambda a: a.reshape(a.shape[0], 1, a.shape[-1])
    w = dict(norm_mix=vec3(norm_mix), w_in=w_in.astype(BF16), conv_w=conv_w, conv_b=vec3(conv_b),
             conv_ln_g=vec3(conv_ln_g), conv_ln_b=vec3(conv_ln_b), w_branch=w_branch.astype(BF16),
             w_o=w_o.astype(BF16), norm_ffn=vec3(norm_ffn), ffn_up=ffn_up.astype(BF16),
             ffn_dw_w=ffn_dw_w, ffn_dw_b=vec3(ffn_dw_b), ffn_down=ffn_down.astype(BF16))
    lam_vecs = tuple(vec3(a) for a in (lambda_q1, lambda_k1, lambda_q2, lambda_k2))
    subln = vec3(diff_subln)
    lam_inits = jnp.asarray([0.8 - 0.6 * math.exp(-0.3 * l) for l in range(depth)], F32)

    pos_p = jnp.arange(lp, dtype=jnp.int32)
    pos_s = past_len + jnp.arange(sr, dtype=jnp.int32)
    rope_p = _rope_tables(pos_p, RET_D) + _rope_tables(pos_p, DIFF_DH)
    rope_s = tuple(jnp.tile(t, (bs, 1)) for t in _rope_tables(pos_s, RET_D) + _rope_tables(pos_s, DIFF_DH))

    cache_kt = jnp.transpose(cache_k, (0, 1, 3, 4, 2))
    cache_v2 = cache_v.reshape(depth, n_pool, PAGE_SIZE * DIFF_HEADS, DIFF_DV)
    conv_pre_s = jnp.pad(state_conv, ((0, 0), (0, 0), (CONV_HALO - (CONV_WIDTH - 1), 0), (0, 0)))
    ffn_pre_s = jnp.pad(state_ffn, ((0, 0), (0, 0), (FFN_HALO - (FFN_CONV_WIDTH - 1), 0), (0, 0)))
    zero_ret = jnp.zeros((bp, RET_HEADS, RET_D, RET_D), F32)
    zero_conv = jnp.zeros((bp, CONV_HALO, CONV_CH), F32)
    zero_ffn = jnp.zeros((bp, FFN_HALO, 2 * D_FF), F32)

    xp = x_prompt.reshape(bp * lp, D_MODEL)
    xs = jnp.pad(x_sample, ((0, 0), (0, sr - ls), (0, 0))).reshape(bs * sr, D_MODEL)

    outs_p, outs_s = [], []
    for layer in range(depth):
        lidx = jnp.full((1,), layer, jnp.int32)
        xp, st_p = _layer(
            lidx, xp, w, rope_p, lam_inits, batch=bp, seq=lp,
            attend=lambda h3: flash_diff_attention(lidx, h3, rope_p[2:], lam_vecs, subln, lam_inits, tq=512),
            ret_state=zero_ret, conv_prefix=zero_conv, ffn_prefix=zero_ffn,
            rows=512, mm_rows=1024, ret_chunk=256, nvalid=256, state_layer=False, rope_in_mm=False)
        xs, st_s = _layer(
            lidx, xs, w, rope_s, lam_inits, batch=bs, seq=sr,
            attend=lambda h3: (decode_diff_attention(lidx, page_table, cache_kt, cache_v2, h3, lam_vecs, subln,
                                                     lam_inits, group=8), None),
            ret_state=state_ret, conv_prefix=conv_pre_s, ffn_prefix=ffn_pre_s,
            rows=sr, mm_rows=bs * sr, ret_chunk=sr, nvalid=ls, state_layer=True, rope_in_mm=True)
        outs_p.append(st_p)
        outs_s.append(st_s)

    one = jnp.zeros((1,), jnp.int32)
    y_prompt = rmsnorm(one, xp, norm_final.reshape(1, 1, D_MODEL), F32).reshape(bp, lp, D_MODEL)
    y_sample = rmsnorm(one, xs, norm_final.reshape(1, 1, D_MODEL), F32).reshape(bs, sr, D_MODEL)[:, :ls]

    def stack(outs, i):
        return jnp.stack([o[i] for o in outs])

    kp = stack(outs_p, 0).reshape(depth, bp, lp, 2 * DIFF_HEADS, DIFF_DH)
    vp = stack(outs_p, 1).reshape(depth, bp, lp, DIFF_HEADS, DIFF_DV)
    rp = stack(outs_p, 2)
    cp = stack(outs_p, 3)[:, :, CONV_HALO - (CONV_WIDTH - 1):]
    fp = stack(outs_p, 4)[:, :, FFN_HALO - (FFN_CONV_WIDTH - 1):]
    ks = stack(outs_s, 0)[:, :, :ls].reshape(depth, bs, ls, 2 * DIFF_HEADS, DIFF_DH)
    vs = stack(outs_s, 1)[:, :, :ls].reshape(depth, bs, ls, DIFF_HEADS, DIFF_DV)
    rs = stack(outs_s, 2)
    cs = stack(outs_s, 3)[:, :, CONV_HALO - (CONV_WIDTH - 1):]
    fs = stack(outs_s, 4)[:, :, FFN_HALO - (FFN_CONV_WIDTH - 1):]
    return (y_prompt, y_sample, kp, vp, rp, cp, fp, ks, vs, rs, cs, fs)
```

```python
import functools
import math

import jax
import jax.numpy as jnp
from jax import lax
from jax.experimental import pallas as pl
from jax.experimental.pallas import tpu as pltpu

F32 = jnp.float32
BF16 = jnp.bfloat16

D_MODEL = 2048
DEPTH = 4
PAGE_SIZE = 128
RET_HEADS = 8
RET_D = 128
DIFF_HEADS = 8
DIFF_DH = 64
DIFF_DV = 128
ROPE_THETA = 10000.0
CONV_CH = 1024
CONV_WIDTH = 31
D_FF = 5632
FFN_CONV_WIDTH = 3
EPS = 1e-6
LANES = 128
SUBLANES = 8
COL = 1024
CB_RQ, CB_RK, CB_RV, CB_RG, CB_DQ, CB_DK, CB_DV, CB_CA, CB_CB, CB_GATE = 0, 1, 2, 3, 4, 5, 6, 7, 8, 9
W_IN_COLS = 15 * COL
NEG = -1e30
VMEM_LIMIT = 56 * 1024 * 1024
SAMPLE_ROWS = 8


def _params(n_grid):
    return pltpu.CompilerParams(dimension_semantics=("arbitrary",) * n_grid,
                                vmem_limit_bytes=VMEM_LIMIT)


def _call(body, *, grid, in_specs, out_specs, out_shape, scratch=(), prefetch, args, name):
    spec = pltpu.PrefetchScalarGridSpec(num_scalar_prefetch=len(prefetch), grid=grid,
                                        in_specs=in_specs, out_specs=out_specs,
                                        scratch_shapes=list(scratch))
    return pl.pallas_call(body, grid_spec=spec, out_shape=out_shape,
                          compiler_params=_params(len(grid)), name=name)(*prefetch, *args)


def _sigmoid(x):
    return 1.0 / (1.0 + jnp.exp(-x))


def _silu(x):
    return x * _sigmoid(x)


def _rmsnorm_body(l_ref, x_ref, g_ref, o_ref):
    x = x_ref[...]
    inv = lax.rsqrt(jnp.mean(x * x, axis=-1, keepdims=True) + EPS)
    o_ref[...] = (x * inv * g_ref[...]).astype(o_ref.dtype)


def rmsnorm(lidx, x, g, out_dtype):
    m, d = x.shape
    tm = min(m, 512)
    return _call(
        _rmsnorm_body, grid=(m // tm,),
        in_specs=[pl.BlockSpec((tm, d), lambda i, l: (i, 0)),
                  pl.BlockSpec((None, 1, d), lambda i, l: (l[0], 0, 0))],
        out_specs=pl.BlockSpec((tm, d), lambda i, l: (i, 0)),
        out_shape=jax.ShapeDtypeStruct((m, d), out_dtype),
        prefetch=(lidx,), args=(x, g), name="rmsnorm")


def _rope_cols(x, cos, sin, head_dim):
    if head_dim == LANES:
        rot = pltpu.roll(x, LANES // 2, 1)
    else:
        half = head_dim // 2
        lane = lax.broadcasted_iota(jnp.int32, x.shape, 1)
        rot = jnp.where((lane % head_dim) < half, pltpu.roll(x, LANES - half, 1), pltpu.roll(x, half, 1))
    return x * cos + rot * sin


def _mm_body(l_ref, x_ref, w_ref, *rest, mode):
    acc = jnp.dot(x_ref[...], w_ref[...], preferred_element_type=F32)
    if mode == "plain":
        (o_ref,) = rest
        o_ref[...] = acc.astype(o_ref.dtype)
    elif mode == "res":
        r_ref, o_ref = rest
        o_ref[...] = r_ref[...] + acc
    else:
        c128, s128, c64, s64, o_ref = rest
        j = pl.program_id(0)
        is128 = jnp.logical_or(j == CB_RQ, j == CB_RK)
        is64 = jnp.logical_or(j == CB_DQ, j == CB_DK)

        @pl.when(is128)
        def _():
            for g in range(COL // LANES):
                cs = slice(g * LANES, (g + 1) * LANES)
                o_ref[:, cs] = _rope_cols(acc[:, cs], c128[...], s128[...], RET_D)

        @pl.when(is64)
        def _():
            for g in range(COL // LANES):
                cs = slice(g * LANES, (g + 1) * LANES)
                o_ref[:, cs] = _rope_cols(acc[:, cs], c64[...], s64[...], DIFF_DH)

        @pl.when(jnp.logical_not(jnp.logical_or(is128, is64)))
        def _():
            o_ref[...] = acc


def matmul(lidx, x, w, *, tn, tm=512, mode="plain", res=None, rope=None, out_dtype=F32):
    m, k = x.shape
    n = w.shape[-1]
    tm = min(m, tm)
    grid = (n // tn, m // tm)
    in_specs = [pl.BlockSpec((tm, k), lambda j, i, l: (i, 0)),
                pl.BlockSpec((None, k, tn), lambda j, i, l: (l[0], 0, j))]
    args = [x, w]
    if mode == "res":
        in_specs.append(pl.BlockSpec((tm, tn), lambda j, i, l: (i, j)))
        args.append(res)
    elif mode == "rope":
        assert tn == COL
        nt = rope[0].shape[0] // tm
        for t in rope:
            in_specs.append(pl.BlockSpec((tm, LANES), lambda j, i, l: (i % nt, 0)))
            args.append(t)
    return _call(
        functools.partial(_mm_body, mode=mode), grid=grid, in_specs=in_specs,
        out_specs=pl.BlockSpec((tm, tn), lambda j, i, l: (i, j)),
        out_shape=jax.ShapeDtypeStruct((m, n), out_dtype),
        prefetch=(lidx,), args=args, name="mm_" + mode)


def _ret_body(l_ref, q_ref, k_ref, v_ref, g_ref, s0_ref, d_ref, xi_ref, zeta_ref, gc_ref, *rest, chunk, rope):
    if rope:
        cos_ref, sin_ref, o_ref, sout_ref, s_scr = rest
    else:
        o_ref, sout_ref, s_scr = rest
    t = pl.program_id(1)

    @pl.when(t == 0)
    def _():
        s_scr[...] = s0_ref[...]

    rows = q_ref.shape[0]
    cp = d_ref.shape[-1]
    nt_dims = (((1,), (1,)), ((), ()))
    tn_dims = (((0,), (0,)), ((), ()))
    for h in range(RET_HEADS):
        cs = slice(h * RET_D, (h + 1) * RET_D)
        for c in range(rows // chunk):
            rs = slice(c * chunk, (c + 1) * chunk)
            q = q_ref[rs, cs]
            k = k_ref[rs, cs]
            if rope:
                q = _rope_cols(q, cos_ref[rs], sin_ref[rs], RET_D)
                k = _rope_cols(k, cos_ref[rs], sin_ref[rs], RET_D)
            k = k * (RET_D ** -0.5)
            v = v_ref[rs, cs]
            if cp > chunk:
                pad = jnp.zeros((cp - chunk, RET_D), F32)
                q, k, v = (jnp.concatenate([a, pad], axis=0) for a in (q, k, v))
            qb, kb, vb = q.astype(BF16), k.astype(BF16), v.astype(BF16)
            inner = lax.dot_general(qb, kb, nt_dims, preferred_element_type=F32) * d_ref[h]
            s = s_scr[h]
            o = (jnp.dot(inner.astype(BF16), vb, preferred_element_type=F32)
                 + jnp.dot(qb, s.astype(BF16), preferred_element_type=F32) * xi_ref[h])
            kz = (k * zeta_ref[h]).astype(BF16)
            s_scr[h] = s * gc_ref[h] + lax.dot_general(kz, vb, tn_dims, preferred_element_type=F32)
            o = o[:chunk]
            o = o * lax.rsqrt(jnp.mean(o * o, axis=-1, keepdims=True) + EPS)
            o_ref[rs, cs] = (o * _silu(g_ref[rs, cs])).astype(o_ref.dtype)

    @pl.when(t == pl.num_programs(1) - 1)
    def _():
        sout_ref[...] = s_scr[...]


def _ret_consts(cp, nvalid):
    log_g = jnp.log1p(-jnp.exp2(-5.0 - jnp.arange(RET_HEADS, dtype=F32)))
    idx = jnp.arange(cp, dtype=F32)
    dist = idx[:, None] - idx[None, :]
    d = jnp.where(dist >= 0, jnp.exp(log_g[:, None, None] * jnp.maximum(dist, 0.0)), 0.0)
    xi = jnp.exp(log_g[:, None] * (idx + 1.0))[:, :, None]
    zeta = jnp.where(idx < nvalid, jnp.exp(log_g[:, None] * (nvalid - 1.0 - idx)), 0.0)[:, :, None]
    gc = jnp.exp(log_g * nvalid)[:, None, None]
    return d.astype(F32), xi.astype(F32), zeta.astype(F32), gc.astype(F32)


def retention(lidx, h3, s0, *, rows, chunk, nvalid, s0_layer, rope=None):
    b, seq, _ = h3.shape
    cp = max(chunk, LANES)
    d, xi, zeta, gc = _ret_consts(cp, nvalid)

    def hspec(cb):
        return pl.BlockSpec((None, rows, COL), lambda bi, t, l: (bi, t, cb))

    if s0_layer:
        s0_spec = pl.BlockSpec((None, None, RET_HEADS, RET_D, RET_D), lambda bi, t, l: (l[0], bi, 0, 0, 0))
    else:
        s0_spec = pl.BlockSpec((None, RET_HEADS, RET_D, RET_D), lambda bi, t, l: (bi, 0, 0, 0))
    full3 = lambda a: pl.BlockSpec(a.shape, lambda bi, t, l: (0, 0, 0))
    tabs = list(rope) if rope else []
    return _call(
        functools.partial(_ret_body, chunk=chunk, rope=bool(rope)), grid=(b, seq // rows),
        in_specs=[hspec(CB_RQ), hspec(CB_RK), hspec(CB_RV), hspec(CB_RG), s0_spec,
                  full3(d), full3(xi), full3(zeta), full3(gc)]
        + [pl.BlockSpec((rows, LANES), lambda bi, t, l: (t, 0)) for _ in tabs],
        out_specs=[pl.BlockSpec((None, rows, COL), lambda bi, t, l: (bi, t, 0)),
                   pl.BlockSpec((None, RET_HEADS, RET_D, RET_D), lambda bi, t, l: (bi, 0, 0, 0))],
        out_shape=[jax.ShapeDtypeStruct((b, seq, COL), BF16),
                   jax.ShapeDtypeStruct((b, RET_HEADS, RET_D, RET_D), F32)],
        scratch=[pltpu.VMEM((RET_HEADS, RET_D, RET_D), F32)],
        prefetch=(lidx,), args=(h3, h3, h3, h3, s0, d, xi, zeta, gc, *tabs), name="retention")


def _lambda(lq1, lk1, lq2, lk2, lam_init):
    a = jnp.exp(jnp.sum(lq1[...] * lk1[...], axis=-1, keepdims=True))
    b = jnp.exp(jnp.sum(lq2[...] * lk2[...], axis=-1, keepdims=True))
    return a - b + lam_init


def _diff_out(o, sub_ref, lam_init):
    o = o * lax.rsqrt(jnp.mean(o * o, axis=-1, keepdims=True) + EPS)
    return o * sub_ref[...] * (1.0 - lam_init)


def _softmax_step(s, vb, m, l, acc):
    m_new = jnp.maximum(m, jnp.max(s, axis=-1, keepdims=True))
    p = jnp.exp(s - m_new)
    alpha = jnp.exp(m - m_new)
    l = alpha * l + jnp.sum(p, axis=-1, keepdims=True)
    acc = alpha * acc + jnp.dot(p.astype(BF16), vb, preferred_element_type=F32)
    return m_new, l, acc


FLASH_RG = 32
LOG2E = math.log2(math.e)


def _flash_body(l_ref, q_ref, k_ref, v_ref, cq_ref, sq_ref, ck_ref, sk_ref, lq1, lk1, lq2, lk2, sub_ref, li_ref,
                o_ref, kr_ref, kb_scr, vb_scr, q2_scr, s_scr, p_scr, m_scr, a_scr, acc_scr, *, tq):
    qi = pl.program_id(2)
    nt_dims = (((1,), (1,)), ((), ()))

    @pl.when(qi == 0)
    def _():
        kr = _rope_cols(k_ref[...], ck_ref[...], sk_ref[...], DIFF_DH)
        kr_ref[...] = kr
        kb_scr[...] = kr.astype(BF16)
        vb_scr[:, 0:DIFF_DV] = v_ref[...].astype(BF16)
        vb_scr[:, DIFF_DV:2 * DIFF_DV] = jnp.ones((vb_scr.shape[0], DIFF_DV), BF16)

    q = _rope_cols(q_ref[...], cq_ref[...], sq_ref[...], DIFF_DH) * (DIFF_DH ** -0.5 * LOG2E)
    lane = lax.broadcasted_iota(jnp.int32, q.shape, 1)
    q2_scr[0:tq] = jnp.where(lane < DIFF_DH, q, 0.0).astype(BF16)
    q2_scr[tq:2 * tq] = jnp.where(lane >= DIFF_DH, q, 0.0).astype(BF16)
    m_scr[...] = jnp.full(m_scr.shape, NEG, F32)
    acc_scr[...] = jnp.zeros(acc_scr.shape, F32)

    def scores(j, slot):
        off = pl.multiple_of(j * tq, tq)
        s_scr[slot] = lax.dot_general(q2_scr[...], kb_scr[pl.ds(off, tq), :], nt_dims,
                                      preferred_element_type=F32)

    def softmax_pv(j, slot, masked):
        off = pl.multiple_of(j * tq, tq)
        for r0 in range(0, 2 * tq, FLASH_RG):
            rs = slice(r0, r0 + FLASH_RG)
            s = s_scr[slot, rs, :]
            if masked:
                row = lax.broadcasted_iota(jnp.int32, s.shape, 0) + (r0 % tq)
                col = lax.broadcasted_iota(jnp.int32, s.shape, 1)
                s = jnp.where(col <= row, s, NEG)
            m_old = m_scr[rs]
            m_new = jnp.maximum(m_old, jnp.broadcast_to(jnp.max(s, axis=-1, keepdims=True), m_old.shape))
            p = jnp.exp2(s - pltpu.repeat(m_new, tq // LANES, axis=1))
            a_scr[rs] = jnp.exp2(m_old - m_new)
            m_scr[rs] = m_new
            p_scr[rs, :] = p.astype(BF16)
        vb = vb_scr[pl.ds(off, tq), :]
        for r0 in range(0, 2 * tq, tq):
            rs = slice(r0, r0 + tq)
            acc_scr[rs] = (pltpu.repeat(a_scr[rs], 2, axis=1) * acc_scr[rs]
                           + jnp.dot(p_scr[rs, :], vb, preferred_element_type=F32))

    scores(0, 0)

    def pair(jj, carry):
        j = 2 * jj
        scores(j + 1, 1)
        softmax_pv(j, 0, False)
        scores(j + 2, 0)
        softmax_pv(j + 1, 1, False)
        return carry

    lax.fori_loop(0, qi // 2, pair, 0)

    @pl.when(qi % 2 == 0)
    def _():
        softmax_pv(qi, 0, True)

    @pl.when(qi % 2 == 1)
    def _():
        scores(qi, 1)
        softmax_pv(qi - 1, 0, False)
        softmax_pv(qi, 1, True)

    lam_init = li_ref[l_ref[0]]
    lam = _lambda(lq1, lk1, lq2, lk2, lam_init)
    on = acc_scr[:, 0:DIFF_DV] / acc_scr[:, DIFF_DV:2 * DIFF_DV]
    o = on[:tq] - lam * on[tq:]
    o_ref[...] = _diff_out(o, sub_ref, lam_init).astype(o_ref.dtype)


def _lam_specs(nidx):
    def vec(width):
        if nidx == 3:
            return pl.BlockSpec((None, 1, width), lambda a, b, c, l: (l[0], 0, 0))
        return pl.BlockSpec((None, 1, width), lambda a, b, l, pt: (l[0], 0, 0))
    return [vec(DIFF_DH)] * 4 + [vec(DIFF_DV), pl.BlockSpec(memory_space=pltpu.SMEM)]


def flash_diff_attention(lidx, h3, rope, lam_vecs, subln, lam_inits, *, tq):
    b, seq, _ = h3.shape
    per = COL // LANES
    cos, sin = rope

    def kvspec(cb):
        return pl.BlockSpec((None, seq, LANES), lambda bi, p, qi, l: (bi, 0, cb * per + p))

    q_tab = pl.BlockSpec((tq, LANES), lambda bi, p, qi, l: (qi, 0))
    k_tab = pl.BlockSpec((seq, LANES), lambda bi, p, qi, l: (0, 0))
    return _call(
        functools.partial(_flash_body, tq=tq), grid=(b, DIFF_HEADS, seq // tq),
        in_specs=[pl.BlockSpec((None, tq, LANES), lambda bi, p, qi, l: (bi, qi, CB_DQ * per + p)),
                  kvspec(CB_DK), kvspec(CB_DV), q_tab, q_tab, k_tab, k_tab] + _lam_specs(3),
        out_specs=[pl.BlockSpec((None, tq, LANES), lambda bi, p, qi, l: (bi, qi, p)),
                   pl.BlockSpec((None, seq, LANES), lambda bi, p, qi, l: (bi, 0, p))],
        out_shape=[jax.ShapeDtypeStruct((b, seq, COL), BF16), jax.ShapeDtypeStruct((b, seq, COL), F32)],
        scratch=[pltpu.VMEM((seq, LANES), BF16), pltpu.VMEM((seq, 2 * DIFF_DV), BF16),
                 pltpu.VMEM((2 * tq, LANES), BF16), pltpu.VMEM((2, 2 * tq, tq), F32),
                 pltpu.VMEM((2 * tq, tq), BF16), pltpu.VMEM((2 * tq, LANES), F32),
                 pltpu.VMEM((2 * tq, LANES), F32), pltpu.VMEM((2 * tq, 2 * DIFF_DV), F32)],
        prefetch=(lidx,), args=(h3, h3, h3, cos, sin, cos, sin, *lam_vecs, subln, lam_inits), name="flash_diff")


def _decode_body(l_ref, pt_ref, *refs, group):
    k_pages = refs[:group]
    v_pages = refs[group:2 * group]
    (q_ref, kn_ref, vn_ref, lq1, lk1, lq2, lk2, sub_ref, li_ref,
     o_ref, qbd_scr, m_scr, l_scr, acc_scr) = refs[2 * group:]
    s_id = pl.program_id(1)
    pair_rows = 2 * SAMPLE_ROWS
    nt_dims = (((1,), (1,)), ((), ()))

    @pl.when(s_id == 0)
    def _():
        q = q_ref[...] * (DIFF_DH ** -0.5)
        qt = jnp.concatenate([q] * (2 * DIFF_HEADS), axis=0)
        rh = lax.broadcasted_iota(jnp.int32, qt.shape, 0) // SAMPLE_ROWS
        ch = lax.broadcasted_iota(jnp.int32, qt.shape, 1) // DIFF_DH
        qbd_scr[...] = jnp.where(rh == ch, qt, 0.0).astype(BF16)
        m_scr[...] = jnp.full(m_scr.shape, NEG, F32)
        l_scr[...] = jnp.zeros(l_scr.shape, F32)
        acc_scr[...] = jnp.zeros(acc_scr.shape, F32)

    qbd = qbd_scr[...]
    s = jnp.concatenate(
        [jnp.dot(qbd, kp[...].reshape(COL, PAGE_SIZE).astype(BF16), preferred_element_type=F32)
         for kp in k_pages], axis=1)
    m = m_scr[...]
    m_new = jnp.maximum(m, jnp.max(s, axis=-1, keepdims=True))
    p = jnp.exp(s - m_new).astype(BF16)
    alpha = jnp.exp(m - m_new)
    l_scr[...] = alpha * l_scr[...] + jnp.sum(p.astype(F32), axis=-1, keepdims=True)
    pv = []
    for hp in range(DIFF_HEADS):
        vh = jnp.concatenate([vp[pl.ds(hp, PAGE_SIZE, stride=DIFF_HEADS), :] for vp in v_pages],
                             axis=0).astype(BF16)
        pv.append(jnp.dot(p[hp * pair_rows:(hp + 1) * pair_rows], vh, preferred_element_type=F32))
    acc_scr[...] = alpha * acc_scr[...] + jnp.concatenate(pv, axis=0)
    m_scr[...] = m_new

    @pl.when(s_id == pl.num_programs(1) - 1)
    def _():
        pad = jnp.zeros((LANES - SAMPLE_ROWS, COL), F32)
        kn = jnp.concatenate([kn_ref[...], pad], axis=0).astype(BF16)
        vn = jnp.concatenate([vn_ref[...], pad], axis=0).astype(BF16)
        sn = lax.dot_general(qbd, kn, nt_dims, preferred_element_type=F32)
        tok = lax.broadcasted_iota(jnp.int32, sn.shape, 0) % SAMPLE_ROWS
        key = lax.broadcasted_iota(jnp.int32, sn.shape, 1)
        sn = jnp.where(key <= tok, sn, NEG)
        mo = m_scr[...]
        mf = jnp.maximum(mo, jnp.max(sn, axis=-1, keepdims=True))
        pn = jnp.exp(sn - mf).astype(BF16)
        af = jnp.exp(mo - mf)
        lf = af * l_scr[...] + jnp.sum(pn.astype(F32), axis=-1, keepdims=True)
        pvn = jnp.dot(pn, vn, preferred_element_type=F32)
        pvn = jnp.concatenate([pvn[hp * pair_rows:(hp + 1) * pair_rows, hp * DIFF_DV:(hp + 1) * DIFF_DV]
                               for hp in range(DIFF_HEADS)], axis=0)
        on = (af * acc_scr[...] + pvn) / lf
        lam_init = li_ref[l_ref[0]]
        lam = _lambda(lq1, lk1, lq2, lk2, lam_init)
        for hp in range(DIFF_HEADS):
            r0 = hp * pair_rows
            o = on[r0:r0 + SAMPLE_ROWS] - lam * on[r0 + SAMPLE_ROWS:r0 + pair_rows]
            o_ref[:, hp * DIFF_DV:(hp + 1) * DIFF_DV] = _diff_out(o, sub_ref, lam_init).astype(o_ref.dtype)


def decode_diff_attention(lidx, page_table, cache_kt, cache_v, h3, lam_vecs, subln, lam_inits, *, group):
    b = h3.shape[0]
    n_pages = page_table.shape[1]
    rows = 2 * DIFF_HEADS * SAMPLE_ROWS

    def page_spec(shape, g):
        zeros = (0,) * len(shape)
        return pl.BlockSpec((None, None) + shape, lambda bi, s, l, pt: (l[0], pt[bi, s * group + g]) + zeros)

    def hspec(cb):
        return pl.BlockSpec((None, SAMPLE_ROWS, COL), lambda bi, s, l, pt: (bi, 0, cb))

    k_specs = [page_spec((2 * DIFF_HEADS, DIFF_DH, PAGE_SIZE), g) for g in range(group)]
    v_specs = [page_spec((PAGE_SIZE * DIFF_HEADS, DIFF_DV), g) for g in range(group)]
    return _call(
        functools.partial(_decode_body, group=group), grid=(b, n_pages // group),
        in_specs=k_specs + v_specs + [hspec(CB_DQ), hspec(CB_DK), hspec(CB_DV)] + _lam_specs(2),
        out_specs=pl.BlockSpec((None, SAMPLE_ROWS, COL), lambda bi, s, l, pt: (bi, 0, 0)),
        out_shape=jax.ShapeDtypeStruct((b, SAMPLE_ROWS, COL), BF16),
        scratch=[pltpu.VMEM((rows, COL), BF16), pltpu.VMEM((rows, 1), F32), pltpu.VMEM((rows, 1), F32),
                 pltpu.VMEM((rows, DIFF_DV), F32)],
        prefetch=(lidx, page_table),
        args=([cache_kt] * group + [cache_v] * group + [h3, h3, h3] + list(lam_vecs) + [subln, lam_inits]),
        name="decode_diff")


CONV_HALO = 32
CONV_RC = 128
CONV_CW = 128


def _conv_body(l_ref, ca_ref, cb_ref, pre_ref, w_ref, b_ref, lg_ref, lb_ref, o_ref, st_ref, buf, y_scr,
               *, nvalid_last):
    t = pl.program_id(1)
    rows = ca_ref.shape[0]

    @pl.when(t == 0)
    def _():
        buf[0:CONV_HALO] = pre_ref[...]

    if rows >= CONV_HALO:
        @pl.when(t > 0)
        def _():
            buf[0:CONV_HALO] = buf[rows:rows + CONV_HALO]

    buf[CONV_HALO:CONV_HALO + rows] = ca_ref[...] * _sigmoid(cb_ref[...])
    base = CONV_HALO - (CONV_WIDTH - 1)
    rc = min(rows, CONV_RC)
    for c0 in range(0, CONV_CH, CONV_CW):
        cs = slice(c0, c0 + CONV_CW)
        for r0 in range(0, rows, rc):
            acc = jnp.zeros((rc, CONV_CW), F32) + b_ref[:, cs]
            for ph in range(SUBLANES):
                taps = [j for j in range(CONV_WIDTH) if (base + j) % SUBLANES == ph]
                n = rc + (SUBLANES if ph else 0)
                part = None
                for j in taps:
                    start = r0 + base + j - ph
                    term = buf[start:start + n, cs] * w_ref[j:j + 1, cs]
                    part = term if part is None else part + term
                acc = acc + part[ph:ph + rc]
            y_scr[r0:r0 + rc, cs] = acc
    y = y_scr[...]
    mu = jnp.mean(y, axis=-1, keepdims=True)
    yc = y - mu
    var = jnp.mean(yc * yc, axis=-1, keepdims=True)
    y = yc * lax.rsqrt(var + EPS) * lg_ref[...] + lb_ref[...]
    o_ref[...] = _silu(y).astype(o_ref.dtype)

    @pl.when(t == pl.num_programs(1) - 1)
    def _():
        st_ref[...] = buf[nvalid_last:nvalid_last + CONV_HALO]


def conv_module(lidx, h3, prefix, conv_w, conv_b, ln_g, ln_b, *, rows, nvalid_last, prefix_layer):
    b, seq, _ = h3.shape
    if prefix_layer:
        pre_spec = pl.BlockSpec((None, None, CONV_HALO, CONV_CH), lambda bi, t, l: (l[0], bi, 0, 0))
    else:
        pre_spec = pl.BlockSpec((None, CONV_HALO, CONV_CH), lambda bi, t, l: (bi, 0, 0))
    vec = pl.BlockSpec((None, 1, CONV_CH), lambda bi, t, l: (l[0], 0, 0))
    return _call(
        functools.partial(_conv_body, nvalid_last=nvalid_last), grid=(b, seq // rows),
        in_specs=[pl.BlockSpec((None, rows, COL), lambda bi, t, l: (bi, t, CB_CA)),
                  pl.BlockSpec((None, rows, COL), lambda bi, t, l: (bi, t, CB_CB)),
                  pre_spec,
                  pl.BlockSpec((None, CONV_WIDTH, CONV_CH), lambda bi, t, l: (l[0], 0, 0)),
                  vec, vec, vec],
        out_specs=[pl.BlockSpec((None, rows, CONV_CH), lambda bi, t, l: (bi, t, 0)),
                   pl.BlockSpec((None, CONV_HALO, CONV_CH), lambda bi, t, l: (bi, 0, 0))],
        out_shape=[jax.ShapeDtypeStruct((b, seq, CONV_CH), BF16),
                   jax.ShapeDtypeStruct((b, CONV_HALO, CONV_CH), F32)],
        scratch=[pltpu.VMEM((CONV_HALO + rows, CONV_CH), F32), pltpu.VMEM((rows, CONV_CH), F32)],
        prefetch=(lidx,), args=(h3, h3, prefix, conv_w, conv_b, ln_g, ln_b), name="conv_module")


def _merge_body(l_ref, ro_ref, do_ref, co_ref, w0, w1, w2, g0, g1, g2, o_ref):
    acc = _sigmoid(g0[...]) * jnp.dot(ro_ref[...], w0[...], preferred_element_type=F32)
    acc = acc + _sigmoid(g1[...]) * jnp.dot(do_ref[...], w1[...], preferred_element_type=F32)
    acc = acc + _sigmoid(g2[...]) * jnp.dot(co_ref[...], w2[...], preferred_element_type=F32)
    o_ref[...] = acc.astype(o_ref.dtype)


def merge(lidx, ro, do, co, h, w_branch, *, tm=512, tn=COL):
    m = ro.shape[0]
    tm = min(m, tm)
    nb = D_MODEL // tn
    gate0 = (CB_GATE * COL) // tn
    act = pl.BlockSpec((tm, COL), lambda j, i, l: (i, 0))

    def wspec(r):
        return pl.BlockSpec((None, COL, tn), lambda j, i, l: (l[0], r, j))

    def gspec(r):
        return pl.BlockSpec((tm, tn), lambda j, i, l: (i, gate0 + r * nb + j))

    return _call(
        _merge_body, grid=(nb, m // tm),
        in_specs=[act, act, act, wspec(0), wspec(1), wspec(2), gspec(0), gspec(1), gspec(2)],
        out_specs=pl.BlockSpec((tm, tn), lambda j, i, l: (i, j)),
        out_shape=jax.ShapeDtypeStruct((m, D_MODEL), BF16),
        prefetch=(lidx,), args=(ro, do, co, w_branch, w_branch, w_branch, h, h, h), name="merge")


FFN_TN = 512
FFN_HALO = 8


def _ffn_gate_body(l_ref, ug_ref, uv_ref, pg_ref, pv_ref, wg_ref, wv_ref, bg_ref, bv_ref,
                   o_ref, sg_ref, sv_ref, buf, *, nvalid_last):
    t = pl.program_id(2)
    rows = ug_ref.shape[0]
    ys = []
    for half, (u_ref, p_ref, w_ref, b_ref, s_ref) in enumerate(
            ((ug_ref, pg_ref, wg_ref, bg_ref, sg_ref), (uv_ref, pv_ref, wv_ref, bv_ref, sv_ref))):
        @pl.when(t == 0)
        def _():
            buf[half, 0:FFN_HALO] = p_ref[...]

        if rows >= FFN_HALO:
            @pl.when(t > 0)
            def _():
                buf[half, 0:FFN_HALO] = buf[half, rows:rows + FFN_HALO]

        buf[half, FFN_HALO:FFN_HALO + rows] = u_ref[...]
        base = FFN_HALO - (FFN_CONV_WIDTH - 1)
        y = b_ref[...] + buf[half, base:base + rows] * w_ref[0:1]
        for j in range(1, FFN_CONV_WIDTH):
            y = y + buf[half, base + j:base + j + rows] * w_ref[j:j + 1]
        ys.append(y)

        @pl.when(t == pl.num_programs(2) - 1)
        def _():
            s_ref[...] = buf[half, nvalid_last:nvalid_last + FFN_HALO]

    o_ref[...] = (_silu(ys[0]) * ys[1]).astype(o_ref.dtype)


def ffn_gate(lidx, u3, prefix, dw_w, dw_b, *, rows, nvalid_last, prefix_layer, tn=FFN_TN):
    b, seq, _ = u3.shape
    nc = D_FF // tn

    def pspec(off):
        if prefix_layer:
            return pl.BlockSpec((None, None, FFN_HALO, tn), lambda bi, c, t, l: (l[0], bi, 0, c + off))
        return pl.BlockSpec((None, FFN_HALO, tn), lambda bi, c, t, l: (bi, 0, c + off))

    def uspec(off):
        return pl.BlockSpec((None, rows, tn), lambda bi, c, t, l: (bi, t, c + off))

    def wspec(off):
        return pl.BlockSpec((None, FFN_CONV_WIDTH, tn), lambda bi, c, t, l: (l[0], 0, c + off))

    def bspec(off):
        return pl.BlockSpec((None, 1, tn), lambda bi, c, t, l: (l[0], 0, c + off))

    st_spec = pl.BlockSpec((None, FFN_HALO, tn), lambda bi, c, t, l: (bi, 0, c))
    st_shape = jax.ShapeDtypeStruct((b, FFN_HALO, D_FF), F32)
    return _call(
        functools.partial(_ffn_gate_body, nvalid_last=nvalid_last), grid=(b, nc, seq // rows),
        in_specs=[uspec(0), uspec(nc), pspec(0), pspec(nc), wspec(0), wspec(nc), bspec(0), bspec(nc)],
        out_specs=[pl.BlockSpec((None, rows, tn), lambda bi, c, t, l: (bi, t, c)), st_spec, st_spec],
        out_shape=[jax.ShapeDtypeStruct((b, seq, D_FF), BF16), st_shape, st_shape],
        scratch=[pltpu.VMEM((2, FFN_HALO + rows, tn), F32)],
        prefetch=(lidx,), args=(u3, u3, prefix, prefix, dw_w, dw_w, dw_b, dw_b), name="ffn_gate")


def _up_gate_body(l_ref, x_ref, wg_ref, wv_ref, pg_ref, pv_ref, cg_ref, cv_ref, bg_ref, bv_ref,
                  o_ref, sg_ref, sv_ref, buf, *, nt, split):
    t = pl.program_id(1) % nt
    tm = x_ref.shape[0]
    hm = tm // split
    base = FFN_HALO - (FFN_CONV_WIDTH - 1)
    halves = ((wg_ref, pg_ref, cg_ref, bg_ref, sg_ref), (wv_ref, pv_ref, cv_ref, bv_ref, sv_ref))

    @pl.when(jnp.logical_and(pl.program_id(0) == 0, pl.program_id(1) == 0))
    def _():
        buf[...] = jnp.zeros(buf.shape, F32)

    tails = [jnp.where(t == 0, p_ref[...], buf[half]) for half, (_, p_ref, _, _, _) in enumerate(halves)]
    for r in range(split):
        ys = []
        for half, (w_ref, p_ref, c_ref, b_ref, s_ref) in enumerate(halves):
            u = jnp.dot(x_ref[r * hm:(r + 1) * hm, :], w_ref[...], preferred_element_type=F32)
            ext = jnp.concatenate([tails[half], u], axis=0)
            y = b_ref[...] + u * c_ref[FFN_CONV_WIDTH - 1:FFN_CONV_WIDTH]
            for j in range(FFN_CONV_WIDTH - 1):
                y = y + ext[base + j:base + j + hm] * c_ref[j:j + 1]
            ys.append(y)
            tails[half] = u[hm - FFN_HALO:]
        o_ref[r * hm:(r + 1) * hm, :] = (_silu(ys[0]) * ys[1]).astype(o_ref.dtype)
    for half, (w_ref, p_ref, c_ref, b_ref, s_ref) in enumerate(halves):
        buf[half] = tails[half]
        s_ref[...] = tails[half]


def ffn_up_gate(lidx, xn, w_up, prefix, dw_w, dw_b, *, batch, seq, tm, split):
    m, k = xn.shape
    nc = D_FF // FFN_TN
    nt = seq // tm

    def wspec(off):
        return pl.BlockSpec((None, k, FFN_TN), lambda c, i, l: (l[0], 0, c + off))

    def pspec(off):
        return pl.BlockSpec((None, FFN_HALO, FFN_TN), lambda c, i, l: (i // nt, 0, c + off))

    def cspec(off):
        return pl.BlockSpec((None, FFN_CONV_WIDTH, FFN_TN), lambda c, i, l: (l[0], 0, c + off))

    def bspec(off):
        return pl.BlockSpec((None, 1, FFN_TN), lambda c, i, l: (l[0], 0, c + off))

    st_spec = pl.BlockSpec((None, FFN_HALO, FFN_TN), lambda c, i, l: (i // nt, 0, c))
    st_shape = jax.ShapeDtypeStruct((batch, FFN_HALO, D_FF), F32)
    return _call(
        functools.partial(_up_gate_body, nt=nt, split=split), grid=(nc, m // tm),
        in_specs=[pl.BlockSpec((tm, k), lambda c, i, l: (i, 0)), wspec(0), wspec(nc), pspec(0), pspec(nc),
                  cspec(0), cspec(nc), bspec(0), bspec(nc)],
        out_specs=[pl.BlockSpec((tm, FFN_TN), lambda c, i, l: (i, c)), st_spec, st_spec],
        out_shape=[jax.ShapeDtypeStruct((m, D_FF), BF16), st_shape, st_shape],
        scratch=[pltpu.VMEM((2, FFN_HALO, FFN_TN), F32)],
        prefetch=(lidx,), args=(xn, w_up, w_up, prefix, prefix, dw_w, dw_w, dw_b, dw_b), name="ffn_up_gate")


def _rope_tables(pos, head_dim):
    half = head_dim // 2
    inv = ROPE_THETA ** (-jnp.arange(half, dtype=F32) * 2.0 / head_dim)
    ang = pos.astype(F32)[:, None] * inv[None, :]
    cos, sin = jnp.cos(ang), jnp.sin(ang)
    reps = LANES // head_dim
    return (jnp.tile(jnp.concatenate([cos, cos], axis=-1), (1, reps)),
            jnp.tile(jnp.concatenate([-sin, sin], axis=-1), (1, reps)))


def _layer(lidx, x, w, rope, lam_inits, *, batch, seq, attend, ret_state, conv_prefix, ffn_prefix,
           rows, mm_rows, ret_chunk, nvalid, state_layer, rope_in_mm):
    m = batch * seq
    xn = rmsnorm(lidx, x, w["norm_mix"], BF16)
    if rope_in_mm:
        h = matmul(lidx, xn, w["w_in"], tn=COL, tm=mm_rows, mode="rope", rope=rope)
    else:
        h = matmul(lidx, xn, w["w_in"], tn=COL, tm=mm_rows)
    h3 = h.reshape(batch, seq, W_IN_COLS)
    ro, ret_new = retention(lidx, h3, ret_state, rows=rows, chunk=ret_chunk, nvalid=nvalid, s0_layer=state_layer,
                            rope=None if rope_in_mm else rope[:2])
    do, k_rows = attend(h3)
    co, conv_new = conv_module(lidx, h3, conv_prefix, w["conv_w"], w["conv_b"], w["conv_ln_g"], w["conv_ln_b"],
                               rows=rows, nvalid_last=nvalid if seq == rows else rows,
                               prefix_layer=state_layer)
    mg = merge(lidx, ro.reshape(m, COL), do.reshape(m, COL), co.reshape(m, COL), h, w["w_branch"])
    x = matmul(lidx, mg, w["w_o"], tn=COL, tm=mm_rows, mode="res", res=x)
    xn = rmsnorm(lidx, x, w["norm_ffn"], BF16)
    if seq > rows:
        a, st_g, st_v = ffn_up_gate(lidx, xn, w["ffn_up"], ffn_prefix, w["ffn_dw_w"], w["ffn_dw_b"],
                                    batch=batch, seq=seq, tm=mm_rows, split=4)
    else:
        u = matmul(lidx, xn, w["ffn_up"], tn=COL)
        a, st_g, st_v = ffn_gate(lidx, u.reshape(batch, seq, 2 * D_FF), ffn_prefix, w["ffn_dw_w"], w["ffn_dw_b"],
                                 rows=rows, nvalid_last=nvalid, prefix_layer=state_layer, tn=D_FF)
    x = matmul(lidx, a.reshape(m, D_FF), w["ffn_down"], tn=FFN_TN, tm=mm_rows, mode="res", res=x)
    if k_rows is None:
        k_rows = h3[:, :, CB_DK * COL:(CB_DK + 1) * COL]
    v_rows = h3[:, :, CB_DV * COL:(CB_DV + 1) * COL]
    ffn_new = jnp.concatenate([st_g, st_v], axis=-1)
    return x, (k_rows, v_rows, ret_new, conv_new, ffn_new)


def kernel(x_prompt, x_sample, cache_k, cache_v, page_table, state_ret, state_conv, state_ffn, norm_mix, w_in, lambda_q1, lambda_k1, lambda_q2, lambda_k2, diff_subln, conv_w, conv_b, conv_ln_g, conv_ln_b, w_branch, w_o, norm_ffn, ffn_up, ffn_dw_w, ffn_dw_b, ffn_down, norm_final):
    bp, lp, _ = x_prompt.shape
    bs, ls, _ = x_sample.shape
    depth = w_in.shape[0]
    n_pool = cache_k.shape[1]
    past_len = page_table.shape[1] * PAGE_SIZE
    sr = SAMPLE_ROWS

    vec3 = lambda a: a.reshape(a.shape[0], 1, a.shape[-1])
    w = dict(norm_mix=vec3(norm_mix), w_in=w_in.astype(BF16), conv_w=conv_w, conv_b=vec3(conv_b),
             conv_ln_g=vec3(conv_ln_g), conv_ln_b=vec3(conv_ln_b), w_branch=w_branch.astype(BF16),
             w_o=w_o.astype(BF16), norm_ffn=vec3(norm_ffn), ffn_up=ffn_up.astype(BF16),
             ffn_dw_w=ffn_dw_w, ffn_dw_b=vec3(ffn_dw_b), ffn_down=ffn_down.astype(BF16))
    lam_vecs = tuple(vec3(a) for a in (lambda_q1, lambda_k1, lambda_q2, lambda_k2))
    subln = vec3(diff_subln)
    lam_inits = jnp.asarray([0.8 - 0.6 * math.exp(-0.3 * l) for l in range(depth)], F32)

    pos_p = jnp.arange(lp, dtype=jnp.int32)
    pos_s = past_len + jnp.arange(sr, dtype=jnp.int32)
    rope_p = _rope_tables(pos_p, RET_D) + _rope_tables(pos_p, DIFF_DH)
    rope_s = tuple(jnp.tile(t, (bs, 1)) for t in _rope_tables(pos_s, RET_D) + _rope_tables(pos_s, DIFF_DH))

    cache_kt = jnp.transpose(cache_k, (0, 1, 3, 4, 2))
    cache_v2 = cache_v.reshape(depth, n_pool, PAGE_SIZE * DIFF_HEADS, DIFF_DV)
    conv_pre_s = jnp.pad(state_conv, ((0, 0), (0, 0), (CONV_HALO - (CONV_WIDTH - 1), 0), (0, 0)))
    ffn_pre_s = jnp.pad(state_ffn, ((0, 0), (0, 0), (FFN_HALO - (FFN_CONV_WIDTH - 1), 0), (0, 0)))
    zero_ret = jnp.zeros((bp, RET_HEADS, RET_D, RET_D), F32)
    zero_conv = jnp.zeros((bp, CONV_HALO, CONV_CH), F32)
    zero_ffn = jnp.zeros((bp, FFN_HALO, 2 * D_FF), F32)

    xp = x_prompt.reshape(bp * lp, D_MODEL)
    xs = jnp.pad(x_sample, ((0, 0), (0, sr - ls), (0, 0))).reshape(bs * sr, D_MODEL)

    outs_p, outs_s = [], []
    for layer in range(depth):
        lidx = jnp.full((1,), layer, jnp.int32)
        xp, st_p = _layer(
            lidx, xp, w, rope_p, lam_inits, batch=bp, seq=lp,
            attend=lambda h3: flash_diff_attention(lidx, h3, rope_p[2:], lam_vecs, subln, lam_inits, tq=512),
            ret_state=zero_ret, conv_prefix=zero_conv, ffn_prefix=zero_ffn,
            rows=512, mm_rows=1024, ret_chunk=128, nvalid=128, state_layer=False, rope_in_mm=False)
        xs, st_s = _layer(
            lidx, xs, w, rope_s, lam_inits, batch=bs, seq=sr,
            attend=lambda h3: (decode_diff_attention(lidx, page_table, cache_kt, cache_v2, h3, lam_vecs, subln,
                                                     lam_inits, group=16), None),
            ret_state=state_ret, conv_prefix=conv_pre_s, ffn_prefix=ffn_pre_s,
            rows=sr, mm_rows=bs * sr, ret_chunk=sr, nvalid=ls, state_layer=True, rope_in_mm=True)
        outs_p.append(st_p)
        outs_s.append(st_s)

    one = jnp.zeros((1,), jnp.int32)
    y_prompt = rmsnorm(one, xp, norm_final.reshape(1, 1, D_MODEL), F32).reshape(bp, lp, D_MODEL)
    y_sample = rmsnorm(one, xs, norm_final.reshape(1, 1, D_MODEL), F32).reshape(bs, sr, D_MODEL)[:, :ls]

    def stack(outs, i):
        return jnp.stack([o[i] for o in outs])

    kp = stack(outs_p, 0).reshape(depth, bp, lp, 2 * DIFF_HEADS, DIFF_DH)
    vp = stack(outs_p, 1).reshape(depth, bp, lp, DIFF_HEADS, DIFF_DV)
    rp = stack(outs_p, 2)
    cp = stack(outs_p, 3)[:, :, CONV_HALO - (CONV_WIDTH - 1):]
    fp = stack(outs_p, 4)[:, :, FFN_HALO - (FFN_CONV_WIDTH - 1):]
    ks = stack(outs_s, 0)[:, :, :ls].reshape(depth, bs, ls, 2 * DIFF_HEADS, DIFF_DH)
    vs = stack(outs_s, 1)[:, :, :ls].reshape(depth, bs, ls, DIFF_HEADS, DIFF_DV)
    rs = stack(outs_s, 2)
    cs = stack(outs_s, 3)[:, :, CONV_HALO - (CONV_WIDTH - 1):]
    fs = stack(outs_s, 4)[:, :, FFN_HALO - (FFN_CONV_WIDTH - 1):]
    return (y_prompt, y_sample, kp, vp, rp, cp, fp, ks, vs, rs, cs, fs)
```

```python
import functools
import math

import jax
import jax.numpy as jnp
from jax import lax
from jax.experimental import pallas as pl
from jax.experimental.pallas import tpu as pltpu

F32 = jnp.float32
BF16 = jnp.bfloat16

D_MODEL = 2048
DEPTH = 4
PAGE_SIZE = 128
RET_HEADS = 8
RET_D = 128
DIFF_HEADS = 8
DIFF_DH = 64
DIFF_DV = 128
ROPE_THETA = 10000.0
CONV_CH = 1024
CONV_WIDTH = 31
D_FF = 5632
FFN_CONV_WIDTH = 3
EPS = 1e-6
LANES = 128
SUBLANES = 8
COL = 1024
CB_RQ, CB_RK, CB_RV, CB_RG, CB_DQ, CB_DK, CB_DV, CB_CA, CB_CB, CB_GATE = 0, 1, 2, 3, 4, 5, 6, 7, 8, 9
W_IN_COLS = 15 * COL
NEG = -1e30
VMEM_LIMIT = 56 * 1024 * 1024
SAMPLE_ROWS = 8


def _params(n_grid):
    return pltpu.CompilerParams(dimension_semantics=("arbitrary",) * n_grid,
                                vmem_limit_bytes=VMEM_LIMIT)


def _call(body, *, grid, in_specs, out_specs, out_shape, scratch=(), prefetch, args, name):
    spec = pltpu.PrefetchScalarGridSpec(num_scalar_prefetch=len(prefetch), grid=grid,
                                        in_specs=in_specs, out_specs=out_specs,
                                        scratch_shapes=list(scratch))
    return pl.pallas_call(body, grid_spec=spec, out_shape=out_shape,
                          compiler_params=_params(len(grid)), name=name)(*prefetch, *args)


def _sigmoid(x):
    return 1.0 / (1.0 + jnp.exp(-x))


def _silu(x):
    return x * _sigmoid(x)


def _rmsnorm_body(l_ref, x_ref, g_ref, o_ref):
    x = x_ref[...]
    inv = lax.rsqrt(jnp.mean(x * x, axis=-1, keepdims=True) + EPS)
    o_ref[...] = (x * inv * g_ref[...]).astype(o_ref.dtype)


def rmsnorm(lidx, x, g, out_dtype):
    m, d = x.shape
    tm = min(m, 512)
    return _call(
        _rmsnorm_body, grid=(m // tm,),
        in_specs=[pl.BlockSpec((tm, d), lambda i, l: (i, 0)),
                  pl.BlockSpec((None, 1, d), lambda i, l: (l[0], 0, 0))],
        out_specs=pl.BlockSpec((tm, d), lambda i, l: (i, 0)),
        out_shape=jax.ShapeDtypeStruct((m, d), out_dtype),
        prefetch=(lidx,), args=(x, g), name="rmsnorm")


def _rope_cols(x, cos, sin, head_dim):
    if head_dim == LANES:
        rot = pltpu.roll(x, LANES // 2, 1)
    else:
        half = head_dim // 2
        lane = lax.broadcasted_iota(jnp.int32, x.shape, 1)
        rot = jnp.where((lane % head_dim) < half, pltpu.roll(x, LANES - half, 1), pltpu.roll(x, half, 1))
    return x * cos + rot * sin


def _mm_body(l_ref, x_ref, w_ref, *rest, mode):
    acc = jnp.dot(x_ref[...], w_ref[...], preferred_element_type=F32)
    if mode == "plain":
        (o_ref,) = rest
        o_ref[...] = acc.astype(o_ref.dtype)
    elif mode == "res":
        r_ref, o_ref = rest
        o_ref[...] = r_ref[...] + acc
    else:
        c128, s128, c64, s64, o_ref = rest
        j = pl.program_id(0)
        is128 = jnp.logical_or(j == CB_RQ, j == CB_RK)
        is64 = jnp.logical_or(j == CB_DQ, j == CB_DK)

        @pl.when(is128)
        def _():
            for g in range(COL // LANES):
                cs = slice(g * LANES, (g + 1) * LANES)
                o_ref[:, cs] = _rope_cols(acc[:, cs], c128[...], s128[...], RET_D)

        @pl.when(is64)
        def _():
            for g in range(COL // LANES):
                cs = slice(g * LANES, (g + 1) * LANES)
                o_ref[:, cs] = _rope_cols(acc[:, cs], c64[...], s64[...], DIFF_DH)

        @pl.when(jnp.logical_not(jnp.logical_or(is128, is64)))
        def _():
            o_ref[...] = acc


def matmul(lidx, x, w, *, tn, tm=512, mode="plain", res=None, rope=None, out_dtype=F32):
    m, k = x.shape
    n = w.shape[-1]
    tm = min(m, tm)
    grid = (n // tn, m // tm)
    in_specs = [pl.BlockSpec((tm, k), lambda j, i, l: (i, 0)),
                pl.BlockSpec((None, k, tn), lambda j, i, l: (l[0], 0, j))]
    args = [x, w]
    if mode == "res":
        in_specs.append(pl.BlockSpec((tm, tn), lambda j, i, l: (i, j)))
        args.append(res)
    elif mode == "rope":
        assert tn == COL
        nt = rope[0].shape[0] // tm
        for t in rope:
            in_specs.append(pl.BlockSpec((tm, LANES), lambda j, i, l: (i % nt, 0)))
            args.append(t)
    return _call(
        functools.partial(_mm_body, mode=mode), grid=grid, in_specs=in_specs,
        out_specs=pl.BlockSpec((tm, tn), lambda j, i, l: (i, j)),
        out_shape=jax.ShapeDtypeStruct((m, n), out_dtype),
        prefetch=(lidx,), args=args, name="mm_" + mode)


def _ret_body(l_ref, q_ref, k_ref, v_ref, g_ref, s0_ref, d_ref, xi_ref, zeta_ref, gc_ref, *rest, chunk, rope):
    if rope:
        cos_ref, sin_ref, o_ref, sout_ref, s_scr = rest
    else:
        o_ref, sout_ref, s_scr = rest
    t = pl.program_id(1)

    @pl.when(t == 0)
    def _():
        s_scr[...] = s0_ref[...]

    rows = q_ref.shape[0]
    cp = d_ref.shape[-1]
    nt_dims = (((1,), (1,)), ((), ()))
    tn_dims = (((0,), (0,)), ((), ()))
    for h in range(RET_HEADS):
        cs = slice(h * RET_D, (h + 1) * RET_D)
        for c in range(rows // chunk):
            rs = slice(c * chunk, (c + 1) * chunk)
            q = q_ref[rs, cs]
            k = k_ref[rs, cs]
            if rope:
                q = _rope_cols(q, cos_ref[rs], sin_ref[rs], RET_D)
                k = _rope_cols(k, cos_ref[rs], sin_ref[rs], RET_D)
            k = k * (RET_D ** -0.5)
            v = v_ref[rs, cs]
            if cp > chunk:
                pad = jnp.zeros((cp - chunk, RET_D), F32)
                q, k, v = (jnp.concatenate([a, pad], axis=0) for a in (q, k, v))
            qb, kb, vb = q.astype(BF16), k.astype(BF16), v.astype(BF16)
            inner = lax.dot_general(qb, kb, nt_dims, preferred_element_type=F32) * d_ref[h]
            s = s_scr[h]
            o = (jnp.dot(inner.astype(BF16), vb, preferred_element_type=F32)
                 + jnp.dot(qb, s.astype(BF16), preferred_element_type=F32) * xi_ref[h])
            kz = (k * zeta_ref[h]).astype(BF16)
            s_scr[h] = s * gc_ref[h] + lax.dot_general(kz, vb, tn_dims, preferred_element_type=F32)
            o = o[:chunk]
            o = o * lax.rsqrt(jnp.mean(o * o, axis=-1, keepdims=True) + EPS)
            o_ref[rs, cs] = (o * _silu(g_ref[rs, cs])).astype(o_ref.dtype)

    @pl.when(t == pl.num_programs(1) - 1)
    def _():
        sout_ref[...] = s_scr[...]


def _ret_consts(cp, nvalid):
    log_g = jnp.log1p(-jnp.exp2(-5.0 - jnp.arange(RET_HEADS, dtype=F32)))
    idx = jnp.arange(cp, dtype=F32)
    dist = idx[:, None] - idx[None, :]
    d = jnp.where(dist >= 0, jnp.exp(log_g[:, None, None] * jnp.maximum(dist, 0.0)), 0.0)
    xi = jnp.exp(log_g[:, None] * (idx + 1.0))[:, :, None]
    zeta = jnp.where(idx < nvalid, jnp.exp(log_g[:, None] * (nvalid - 1.0 - idx)), 0.0)[:, :, None]
    gc = jnp.exp(log_g * nvalid)[:, None, None]
    return d.astype(F32), xi.astype(F32), zeta.astype(F32), gc.astype(F32)


def retention(lidx, h3, s0, *, rows, chunk, nvalid, s0_layer, rope=None):
    b, seq, _ = h3.shape
    cp = max(chunk, LANES)
    d, xi, zeta, gc = _ret_consts(cp, nvalid)

    def hspec(cb):
        return pl.BlockSpec((None, rows, COL), lambda bi, t, l: (bi, t, cb))

    if s0_layer:
        s0_spec = pl.BlockSpec((None, None, RET_HEADS, RET_D, RET_D), lambda bi, t, l: (l[0], bi, 0, 0, 0))
    else:
        s0_spec = pl.BlockSpec((None, RET_HEADS, RET_D, RET_D), lambda bi, t, l: (bi, 0, 0, 0))
    full3 = lambda a: pl.BlockSpec(a.shape, lambda bi, t, l: (0, 0, 0))
    tabs = list(rope) if rope else []
    return _call(
        functools.partial(_ret_body, chunk=chunk, rope=bool(rope)), grid=(b, seq // rows),
        in_specs=[hspec(CB_RQ), hspec(CB_RK), hspec(CB_RV), hspec(CB_RG), s0_spec,
                  full3(d), full3(xi), full3(zeta), full3(gc)]
        + [pl.BlockSpec((rows, LANES), lambda bi, t, l: (t, 0)) for _ in tabs],
        out_specs=[pl.BlockSpec((None, rows, COL), lambda bi, t, l: (bi, t, 0)),
                   pl.BlockSpec((None, RET_HEADS, RET_D, RET_D), lambda bi, t, l: (bi, 0, 0, 0))],
        out_shape=[jax.ShapeDtypeStruct((b, seq, COL), BF16),
                   jax.ShapeDtypeStruct((b, RET_HEADS, RET_D, RET_D), F32)],
        scratch=[pltpu.VMEM((RET_HEADS, RET_D, RET_D), F32)],
        prefetch=(lidx,), args=(h3, h3, h3, h3, s0, d, xi, zeta, gc, *tabs), name="retention")


def _lambda(lq1, lk1, lq2, lk2, lam_init):
    a = jnp.exp(jnp.sum(lq1[...] * lk1[...], axis=-1, keepdims=True))
    b = jnp.exp(jnp.sum(lq2[...] * lk2[...], axis=-1, keepdims=True))
    return a - b + lam_init


def _diff_out(o, sub_ref, lam_init):
    o = o * lax.rsqrt(jnp.mean(o * o, axis=-1, keepdims=True) + EPS)
    return o * sub_ref[...] * (1.0 - lam_init)


def _softmax_step(s, vb, m, l, acc):
    m_new = jnp.maximum(m, jnp.max(s, axis=-1, keepdims=True))
    p = jnp.exp(s - m_new)
    alpha = jnp.exp(m - m_new)
    l = alpha * l + jnp.sum(p, axis=-1, keepdims=True)
    acc = alpha * acc + jnp.dot(p.astype(BF16), vb, preferred_element_type=F32)
    return m_new, l, acc


FLASH_RG = 32
LOG2E = math.log2(math.e)


def _flash_body(l_ref, q_ref, k_ref, v_ref, cq_ref, sq_ref, ck_ref, sk_ref, lq1, lk1, lq2, lk2, sub_ref, li_ref,
                o_ref, kr_ref, vr_ref, kb_scr, vb_scr, q2_scr, s_scr, p_scr, m_scr, a_scr, acc_scr, *, tq):
    qi = pl.program_id(2)
    nt_dims = (((1,), (1,)), ((), ()))

    @pl.when(qi == 0)
    def _():
        kr = _rope_cols(k_ref[...], ck_ref[...], sk_ref[...], DIFF_DH)
        kr_ref[...] = kr
        kb_scr[...] = kr.astype(BF16)
        vr_ref[...] = v_ref[...]
        vb_scr[:, 0:DIFF_DV] = v_ref[...].astype(BF16)
        vb_scr[:, DIFF_DV:2 * DIFF_DV] = jnp.ones((vb_scr.shape[0], DIFF_DV), BF16)

    q = _rope_cols(q_ref[...], cq_ref[...], sq_ref[...], DIFF_DH) * (DIFF_DH ** -0.5 * LOG2E)
    lane = lax.broadcasted_iota(jnp.int32, q.shape, 1)
    q2_scr[0:tq] = jnp.where(lane < DIFF_DH, q, 0.0).astype(BF16)
    q2_scr[tq:2 * tq] = jnp.where(lane >= DIFF_DH, q, 0.0).astype(BF16)
    m_scr[...] = jnp.full(m_scr.shape, NEG, F32)
    acc_scr[...] = jnp.zeros(acc_scr.shape, F32)

    def scores(j, slot):
        off = pl.multiple_of(j * tq, tq)
        s_scr[slot] = lax.dot_general(q2_scr[...], kb_scr[pl.ds(off, tq), :], nt_dims,
                                      preferred_element_type=F32)

    def softmax_pv(j, slot, masked):
        off = pl.multiple_of(j * tq, tq)
        for r0 in range(0, 2 * tq, FLASH_RG):
            rs = slice(r0, r0 + FLASH_RG)
            s = s_scr[slot, rs, :]
            if masked:
                row = lax.broadcasted_iota(jnp.int32, s.shape, 0) + (r0 % tq)
                col = lax.broadcasted_iota(jnp.int32, s.shape, 1)
                s = jnp.where(col <= row, s, NEG)
            m_old = m_scr[rs]
            m_new = jnp.maximum(m_old, jnp.broadcast_to(jnp.max(s, axis=-1, keepdims=True), m_old.shape))
            p = jnp.exp2(s - pltpu.repeat(m_new, tq // LANES, axis=1))
            a_scr[rs] = jnp.exp2(m_old - m_new)
            m_scr[rs] = m_new
            p_scr[rs, :] = p.astype(BF16)
        vb = vb_scr[pl.ds(off, tq), :]
        for r0 in range(0, 2 * tq, tq):
            rs = slice(r0, r0 + tq)
            acc_scr[rs] = (pltpu.repeat(a_scr[rs], 2, axis=1) * acc_scr[rs]
                           + jnp.dot(p_scr[rs, :], vb, preferred_element_type=F32))

    scores(0, 0)

    def pair(jj, carry):
        j = 2 * jj
        scores(j + 1, 1)
        softmax_pv(j, 0, False)
        scores(j + 2, 0)
        softmax_pv(j + 1, 1, False)
        return carry

    lax.fori_loop(0, qi // 2, pair, 0)

    @pl.when(qi % 2 == 0)
    def _():
        softmax_pv(qi, 0, True)

    @pl.when(qi % 2 == 1)
    def _():
        scores(qi, 1)
        softmax_pv(qi - 1, 0, False)
        softmax_pv(qi, 1, True)

    lam_init = li_ref[l_ref[0]]
    lam = _lambda(lq1, lk1, lq2, lk2, lam_init)
    on = acc_scr[:, 0:DIFF_DV] / acc_scr[:, DIFF_DV:2 * DIFF_DV]
    o = on[:tq] - lam * on[tq:]
    o_ref[...] = _diff_out(o, sub_ref, lam_init).astype(o_ref.dtype)


def _lam_specs(nidx):
    def vec(width):
        if nidx == 3:
            return pl.BlockSpec((None, 1, width), lambda a, b, c, l: (l[0], 0, 0))
        return pl.BlockSpec((None, 1, width), lambda a, b, l, pt: (l[0], 0, 0))
    return [vec(DIFF_DH)] * 4 + [vec(DIFF_DV), pl.BlockSpec(memory_space=pltpu.SMEM)]


def flash_diff_attention(lidx, h3, rope, lam_vecs, subln, lam_inits, *, tq):
    b, seq, _ = h3.shape
    per = COL // LANES
    cos, sin = rope

    def kvspec(cb):
        return pl.BlockSpec((None, seq, LANES), lambda bi, p, qi, l: (bi, 0, cb * per + p))

    q_tab = pl.BlockSpec((tq, LANES), lambda bi, p, qi, l: (qi, 0))
    k_tab = pl.BlockSpec((seq, LANES), lambda bi, p, qi, l: (0, 0))
    return _call(
        functools.partial(_flash_body, tq=tq), grid=(b, DIFF_HEADS, seq // tq),
        in_specs=[pl.BlockSpec((None, tq, LANES), lambda bi, p, qi, l: (bi, qi, CB_DQ * per + p)),
                  kvspec(CB_DK), kvspec(CB_DV), q_tab, q_tab, k_tab, k_tab] + _lam_specs(3),
        out_specs=[pl.BlockSpec((None, tq, LANES), lambda bi, p, qi, l: (bi, qi, p)),
                   pl.BlockSpec((None, seq, LANES), lambda bi, p, qi, l: (bi, 0, p)),
                   pl.BlockSpec((None, seq, LANES), lambda bi, p, qi, l: (bi, 0, p))],
        out_shape=[jax.ShapeDtypeStruct((b, seq, COL), BF16), jax.ShapeDtypeStruct((b, seq, COL), F32),
                   jax.ShapeDtypeStruct((b, seq, COL), F32)],
        scratch=[pltpu.VMEM((seq, LANES), BF16), pltpu.VMEM((seq, 2 * DIFF_DV), BF16),
                 pltpu.VMEM((2 * tq, LANES), BF16), pltpu.VMEM((2, 2 * tq, tq), F32),
                 pltpu.VMEM((2 * tq, tq), BF16), pltpu.VMEM((2 * tq, LANES), F32),
                 pltpu.VMEM((2 * tq, LANES), F32), pltpu.VMEM((2 * tq, 2 * DIFF_DV), F32)],
        prefetch=(lidx,), args=(h3, h3, h3, cos, sin, cos, sin, *lam_vecs, subln, lam_inits), name="flash_diff")


def _decode_body(l_ref, pt_ref, *refs, group):
    k_pages = refs[:group]
    v_pages = refs[group:2 * group]
    (q_ref, kn_ref, vn_ref, lq1, lk1, lq2, lk2, sub_ref, li_ref,
     o_ref, qbd_scr, m_scr, l_scr, acc_scr) = refs[2 * group:]
    s_id = pl.program_id(1)
    pair_rows = 2 * SAMPLE_ROWS
    nt_dims = (((1,), (1,)), ((), ()))

    @pl.when(s_id == 0)
    def _():
        q = q_ref[...] * (DIFF_DH ** -0.5)
        qt = jnp.concatenate([q] * (2 * DIFF_HEADS), axis=0)
        rh = lax.broadcasted_iota(jnp.int32, qt.shape, 0) // SAMPLE_ROWS
        ch = lax.broadcasted_iota(jnp.int32, qt.shape, 1) // DIFF_DH
        qbd_scr[...] = jnp.where(rh == ch, qt, 0.0).astype(BF16)
        m_scr[...] = jnp.full(m_scr.shape, NEG, F32)
        l_scr[...] = jnp.zeros(l_scr.shape, F32)
        acc_scr[...] = jnp.zeros(acc_scr.shape, F32)

    qbd = qbd_scr[...]
    s = jnp.concatenate(
        [jnp.dot(qbd, kp[...].reshape(COL, PAGE_SIZE).astype(BF16), preferred_element_type=F32)
         for kp in k_pages], axis=1)
    m = m_scr[...]
    m_new = jnp.maximum(m, jnp.max(s, axis=-1, keepdims=True))
    p = jnp.exp(s - m_new).astype(BF16)
    alpha = jnp.exp(m - m_new)
    l_scr[...] = alpha * l_scr[...] + jnp.sum(p.astype(F32), axis=-1, keepdims=True)
    pv = []
    for hp in range(DIFF_HEADS):
        vh = jnp.concatenate([vp[pl.ds(hp, PAGE_SIZE, stride=DIFF_HEADS), :] for vp in v_pages],
                             axis=0).astype(BF16)
        pv.append(jnp.dot(p[hp * pair_rows:(hp + 1) * pair_rows], vh, preferred_element_type=F32))
    acc_scr[...] = alpha * acc_scr[...] + jnp.concatenate(pv, axis=0)
    m_scr[...] = m_new

    @pl.when(s_id == pl.num_programs(1) - 1)
    def _():
        pad = jnp.zeros((LANES - SAMPLE_ROWS, COL), F32)
        kn = jnp.concatenate([kn_ref[...], pad], axis=0).astype(BF16)
        vn = jnp.concatenate([vn_ref[...], pad], axis=0).astype(BF16)
        sn = lax.dot_general(qbd, kn, nt_dims, preferred_element_type=F32)
        tok = lax.broadcasted_iota(jnp.int32, sn.shape, 0) % SAMPLE_ROWS
        key = lax.broadcasted_iota(jnp.int32, sn.shape, 1)
        sn = jnp.where(key <= tok, sn, NEG)
        mo = m_scr[...]
        mf = jnp.maximum(mo, jnp.max(sn, axis=-1, keepdims=True))
        pn = jnp.exp(sn - mf).astype(BF16)
        af = jnp.exp(mo - mf)
        lf = af * l_scr[...] + jnp.sum(pn.astype(F32), axis=-1, keepdims=True)
        pvn = jnp.dot(pn, vn, preferred_element_type=F32)
        pvn = jnp.concatenate([pvn[hp * pair_rows:(hp + 1) * pair_rows, hp * DIFF_DV:(hp + 1) * DIFF_DV]
                               for hp in range(DIFF_HEADS)], axis=0)
        on = (af * acc_scr[...] + pvn) / lf
        lam_init = li_ref[l_ref[0]]
        lam = _lambda(lq1, lk1, lq2, lk2, lam_init)
        for hp in range(DIFF_HEADS):
            r0 = hp * pair_rows
            o = on[r0:r0 + SAMPLE_ROWS] - lam * on[r0 + SAMPLE_ROWS:r0 + pair_rows]
            o_ref[:, hp * DIFF_DV:(hp + 1) * DIFF_DV] = _diff_out(o, sub_ref, lam_init).astype(o_ref.dtype)


def decode_diff_attention(lidx, page_table, cache_kt, cache_v, h3, lam_vecs, subln, lam_inits, *, group):
    b = h3.shape[0]
    n_pages = page_table.shape[1]
    rows = 2 * DIFF_HEADS * SAMPLE_ROWS

    def page_spec(shape, g):
        zeros = (0,) * len(shape)
        return pl.BlockSpec((None, None) + shape, lambda bi, s, l, pt: (l[0], pt[bi, s * group + g]) + zeros)

    def hspec(cb):
        return pl.BlockSpec((None, SAMPLE_ROWS, COL), lambda bi, s, l, pt: (bi, 0, cb))

    k_specs = [page_spec((2 * DIFF_HEADS, DIFF_DH, PAGE_SIZE), g) for g in range(group)]
    v_specs = [page_spec((PAGE_SIZE * DIFF_HEADS, DIFF_DV), g) for g in range(group)]
    return _call(
        functools.partial(_decode_body, group=group), grid=(b, n_pages // group),
        in_specs=k_specs + v_specs + [hspec(CB_DQ), hspec(CB_DK), hspec(CB_DV)] + _lam_specs(2),
        out_specs=pl.BlockSpec((None, SAMPLE_ROWS, COL), lambda bi, s, l, pt: (bi, 0, 0)),
        out_shape=jax.ShapeDtypeStruct((b, SAMPLE_ROWS, COL), BF16),
        scratch=[pltpu.VMEM((rows, COL), BF16), pltpu.VMEM((rows, 1), F32), pltpu.VMEM((rows, 1), F32),
                 pltpu.VMEM((rows, DIFF_DV), F32)],
        prefetch=(lidx, page_table),
        args=([cache_kt] * group + [cache_v] * group + [h3, h3, h3] + list(lam_vecs) + [subln, lam_inits]),
        name="decode_diff")


CONV_HALO = 32
CONV_RC = 128
CONV_CW = 128


def _conv_body(l_ref, ca_ref, cb_ref, pre_ref, w_ref, b_ref, lg_ref, lb_ref, o_ref, st_ref, buf, y_scr,
               *, nvalid_last):
    t = pl.program_id(1)
    rows = ca_ref.shape[0]

    @pl.when(t == 0)
    def _():
        buf[0:CONV_HALO] = pre_ref[...]

    if rows >= CONV_HALO:
        @pl.when(t > 0)
        def _():
            buf[0:CONV_HALO] = buf[rows:rows + CONV_HALO]

    buf[CONV_HALO:CONV_HALO + rows] = ca_ref[...] * _sigmoid(cb_ref[...])
    base = CONV_HALO - (CONV_WIDTH - 1)
    rc = min(rows, CONV_RC)
    for c0 in range(0, CONV_CH, CONV_CW):
        cs = slice(c0, c0 + CONV_CW)
        for r0 in range(0, rows, rc):
            acc = jnp.zeros((rc, CONV_CW), F32) + b_ref[:, cs]
            for ph in range(SUBLANES):
                taps = [j for j in range(CONV_WIDTH) if (base + j) % SUBLANES == ph]
                n = rc + (SUBLANES if ph else 0)
                part = None
                for j in taps:
                    start = r0 + base + j - ph
                    term = buf[start:start + n, cs] * w_ref[j:j + 1, cs]
                    part = term if part is None else part + term
                acc = acc + part[ph:ph + rc]
            y_scr[r0:r0 + rc, cs] = acc
    y = y_scr[...]
    mu = jnp.mean(y, axis=-1, keepdims=True)
    yc = y - mu
    var = jnp.mean(yc * yc, axis=-1, keepdims=True)
    y = yc * lax.rsqrt(var + EPS) * lg_ref[...] + lb_ref[...]
    o_ref[...] = _silu(y).astype(o_ref.dtype)

    @pl.when(t == pl.num_programs(1) - 1)
    def _():
        st_ref[...] = buf[nvalid_last:nvalid_last + CONV_HALO]


def conv_module(lidx, h3, prefix, conv_w, conv_b, ln_g, ln_b, *, rows, nvalid_last, prefix_layer):
    b, seq, _ = h3.shape
    if prefix_layer:
        pre_spec = pl.BlockSpec((None, None, CONV_HALO, CONV_CH), lambda bi, t, l: (l[0], bi, 0, 0))
    else:
        pre_spec = pl.BlockSpec((None, CONV_HALO, CONV_CH), lambda bi, t, l: (bi, 0, 0))
    vec = pl.BlockSpec((None, 1, CONV_CH), lambda bi, t, l: (l[0], 0, 0))
    return _call(
        functools.partial(_conv_body, nvalid_last=nvalid_last), grid=(b, seq // rows),
        in_specs=[pl.BlockSpec((None, rows, COL), lambda bi, t, l: (bi, t, CB_CA)),
                  pl.BlockSpec((None, rows, COL), lambda bi, t, l: (bi, t, CB_CB)),
                  pre_spec,
                  pl.BlockSpec((None, CONV_WIDTH, CONV_CH), lambda bi, t, l: (l[0], 0, 0)),
                  vec, vec, vec],
        out_specs=[pl.BlockSpec((None, rows, CONV_CH), lambda bi, t, l: (bi, t, 0)),
                   pl.BlockSpec((None, CONV_HALO, CONV_CH), lambda bi, t, l: (bi, 0, 0))],
        out_shape=[jax.ShapeDtypeStruct((b, seq, CONV_CH), BF16),
                   jax.ShapeDtypeStruct((b, CONV_HALO, CONV_CH), F32)],
        scratch=[pltpu.VMEM((CONV_HALO + rows, CONV_CH), F32), pltpu.VMEM((rows, CONV_CH), F32)],
        prefetch=(lidx,), args=(h3, h3, prefix, conv_w, conv_b, ln_g, ln_b), name="conv_module")


def _merge_body(l_ref, ro_ref, do_ref, co_ref, w0, w1, w2, g0, g1, g2, o_ref):
    acc = _sigmoid(g0[...]) * jnp.dot(ro_ref[...], w0[...], preferred_element_type=F32)
    acc = acc + _sigmoid(g1[...]) * jnp.dot(do_ref[...], w1[...], preferred_element_type=F32)
    acc = acc + _sigmoid(g2[...]) * jnp.dot(co_ref[...], w2[...], preferred_element_type=F32)
    o_ref[...] = acc.astype(o_ref.dtype)


def merge(lidx, ro, do, co, h, w_branch, *, tm=512, tn=COL):
    m = ro.shape[0]
    tm = min(m, tm)
    nb = D_MODEL // tn
    gate0 = (CB_GATE * COL) // tn
    act = pl.BlockSpec((tm, COL), lambda j, i, l: (i, 0))

    def wspec(r):
        return pl.BlockSpec((None, COL, tn), lambda j, i, l: (l[0], r, j))

    def gspec(r):
        return pl.BlockSpec((tm, tn), lambda j, i, l: (i, gate0 + r * nb + j))

    return _call(
        _merge_body, grid=(nb, m // tm),
        in_specs=[act, act, act, wspec(0), wspec(1), wspec(2), gspec(0), gspec(1), gspec(2)],
        out_specs=pl.BlockSpec((tm, tn), lambda j, i, l: (i, j)),
        out_shape=jax.ShapeDtypeStruct((m, D_MODEL), BF16),
        prefetch=(lidx,), args=(ro, do, co, w_branch, w_branch, w_branch, h, h, h), name="merge")


FFN_TN = 512
FFN_HALO = 8


def _ffn_gate_body(l_ref, ug_ref, uv_ref, pg_ref, pv_ref, wg_ref, wv_ref, bg_ref, bv_ref,
                   o_ref, sg_ref, sv_ref, buf, *, nvalid_last):
    t = pl.program_id(2)
    rows = ug_ref.shape[0]
    ys = []
    for half, (u_ref, p_ref, w_ref, b_ref, s_ref) in enumerate(
            ((ug_ref, pg_ref, wg_ref, bg_ref, sg_ref), (uv_ref, pv_ref, wv_ref, bv_ref, sv_ref))):
        @pl.when(t == 0)
        def _():
            buf[half, 0:FFN_HALO] = p_ref[...]

        if rows >= FFN_HALO:
            @pl.when(t > 0)
            def _():
                buf[half, 0:FFN_HALO] = buf[half, rows:rows + FFN_HALO]

        buf[half, FFN_HALO:FFN_HALO + rows] = u_ref[...]
        base = FFN_HALO - (FFN_CONV_WIDTH - 1)
        y = b_ref[...] + buf[half, base:base + rows] * w_ref[0:1]
        for j in range(1, FFN_CONV_WIDTH):
            y = y + buf[half, base + j:base + j + rows] * w_ref[j:j + 1]
        ys.append(y)

        @pl.when(t == pl.num_programs(2) - 1)
        def _():
            s_ref[...] = buf[half, nvalid_last:nvalid_last + FFN_HALO]

    o_ref[...] = (_silu(ys[0]) * ys[1]).astype(o_ref.dtype)


def ffn_gate(lidx, u3, prefix, dw_w, dw_b, *, rows, nvalid_last, prefix_layer, tn=FFN_TN):
    b, seq, _ = u3.shape
    nc = D_FF // tn

    def pspec(off):
        if prefix_layer:
            return pl.BlockSpec((None, None, FFN_HALO, tn), lambda bi, c, t, l: (l[0], bi, 0, c + off))
        return pl.BlockSpec((None, FFN_HALO, tn), lambda bi, c, t, l: (bi, 0, c + off))

    def uspec(off):
        return pl.BlockSpec((None, rows, tn), lambda bi, c, t, l: (bi, t, c + off))

    def wspec(off):
        return pl.BlockSpec((None, FFN_CONV_WIDTH, tn), lambda bi, c, t, l: (l[0], 0, c + off))

    def bspec(off):
        return pl.BlockSpec((None, 1, tn), lambda bi, c, t, l: (l[0], 0, c + off))

    st_spec = pl.BlockSpec((None, FFN_HALO, tn), lambda bi, c, t, l: (bi, 0, c))
    st_shape = jax.ShapeDtypeStruct((b, FFN_HALO, D_FF), F32)
    return _call(
        functools.partial(_ffn_gate_body, nvalid_last=nvalid_last), grid=(b, nc, seq // rows),
        in_specs=[uspec(0), uspec(nc), pspec(0), pspec(nc), wspec(0), wspec(nc), bspec(0), bspec(nc)],
        out_specs=[pl.BlockSpec((None, rows, tn), lambda bi, c, t, l: (bi, t, c)), st_spec, st_spec],
        out_shape=[jax.ShapeDtypeStruct((b, seq, D_FF), BF16), st_shape, st_shape],
        scratch=[pltpu.VMEM((2, FFN_HALO + rows, tn), F32)],
        prefetch=(lidx,), args=(u3, u3, prefix, prefix, dw_w, dw_w, dw_b, dw_b), name="ffn_gate")


def _up_gate_body(l_ref, x_ref, wg_ref, wv_ref, pg_ref, pv_ref, cg_ref, cv_ref, bg_ref, bv_ref,
                  o_ref, sg_ref, sv_ref, buf, *, nt, split):
    t = pl.program_id(1) % nt
    tm = x_ref.shape[0]
    hm = tm // split
    base = FFN_HALO - (FFN_CONV_WIDTH - 1)
    halves = ((wg_ref, pg_ref, cg_ref, bg_ref, sg_ref), (wv_ref, pv_ref, cv_ref, bv_ref, sv_ref))

    @pl.when(jnp.logical_and(pl.program_id(0) == 0, pl.program_id(1) == 0))
    def _():
        buf[...] = jnp.zeros(buf.shape, F32)

    tails = [jnp.where(t == 0, p_ref[...], buf[half]) for half, (_, p_ref, _, _, _) in enumerate(halves)]
    for r in range(split):
        ys = []
        for half, (w_ref, p_ref, c_ref, b_ref, s_ref) in enumerate(halves):
            u = jnp.dot(x_ref[r * hm:(r + 1) * hm, :], w_ref[...], preferred_element_type=F32)
            ext = jnp.concatenate([tails[half], u], axis=0)
            y = b_ref[...] + u * c_ref[FFN_CONV_WIDTH - 1:FFN_CONV_WIDTH]
            for j in range(FFN_CONV_WIDTH - 1):
                y = y + ext[base + j:base + j + hm] * c_ref[j:j + 1]
            ys.append(y)
            tails[half] = u[hm - FFN_HALO:]
        o_ref[r * hm:(r + 1) * hm, :] = (_silu(ys[0]) * ys[1]).astype(o_ref.dtype)
    for half, (w_ref, p_ref, c_ref, b_ref, s_ref) in enumerate(halves):
        buf[half] = tails[half]
        s_ref[...] = tails[half]


def ffn_up_gate(lidx, xn, w_up, prefix, dw_w, dw_b, *, batch, seq, tm, split):
    m, k = xn.shape
    nc = D_FF // FFN_TN
    nt = seq // tm

    def wspec(off):
        return pl.BlockSpec((None, k, FFN_TN), lambda c, i, l: (l[0], 0, c + off))

    def pspec(off):
        return pl.BlockSpec((None, FFN_HALO, FFN_TN), lambda c, i, l: (i // nt, 0, c + off))

    def cspec(off):
        return pl.BlockSpec((None, FFN_CONV_WIDTH, FFN_TN), lambda c, i, l: (l[0], 0, c + off))

    def bspec(off):
        return pl.BlockSpec((None, 1, FFN_TN), lambda c, i, l: (l[0], 0, c + off))

    st_spec = pl.BlockSpec((None, FFN_HALO, FFN_TN), lambda c, i, l: (i // nt, 0, c))
    st_shape = jax.ShapeDtypeStruct((batch, FFN_HALO, D_FF), F32)
    return _call(
        functools.partial(_up_gate_body, nt=nt, split=split), grid=(nc, m // tm),
        in_specs=[pl.BlockSpec((tm, k), lambda c, i, l: (i, 0)), wspec(0), wspec(nc), pspec(0), pspec(nc),
                  cspec(0), cspec(nc), bspec(0), bspec(nc)],
        out_specs=[pl.BlockSpec((tm, FFN_TN), lambda c, i, l: (i, c)), st_spec, st_spec],
        out_shape=[jax.ShapeDtypeStruct((m, D_FF), BF16), st_shape, st_shape],
        scratch=[pltpu.VMEM((2, FFN_HALO, FFN_TN), F32)],
        prefetch=(lidx,), args=(xn, w_up, w_up, prefix, prefix, dw_w, dw_w, dw_b, dw_b), name="ffn_up_gate")


def _rope_tables(pos, head_dim):
    half = head_dim // 2
    inv = ROPE_THETA ** (-jnp.arange(half, dtype=F32) * 2.0 / head_dim)
    ang = pos.astype(F32)[:, None] * inv[None, :]
    cos, sin = jnp.cos(ang), jnp.sin(ang)
    reps = LANES // head_dim
    return (jnp.tile(jnp.concatenate([cos, cos], axis=-1), (1, reps)),
            jnp.tile(jnp.concatenate([-sin, sin], axis=-1), (1, reps)))


def _layer(lidx, x, w, rope, lam_inits, *, batch, seq, attend, ret_state, conv_prefix, ffn_prefix,
           rows, mm_rows, ret_chunk, nvalid, state_layer, rope_in_mm):
    m = batch * seq
    xn = rmsnorm(lidx, x, w["norm_mix"], BF16)
    if rope_in_mm:
        h = matmul(lidx, xn, w["w_in"], tn=COL, tm=mm_rows, mode="rope", rope=rope)
    else:
        h = matmul(lidx, xn, w["w_in"], tn=COL, tm=mm_rows)
    h3 = h.reshape(batch, seq, W_IN_COLS)
    ro, ret_new = retention(lidx, h3, ret_state, rows=rows, chunk=ret_chunk, nvalid=nvalid, s0_layer=state_layer,
                            rope=None if rope_in_mm else rope[:2])
    do, k_rows, v_rows = attend(h3)
    co, conv_new = conv_module(lidx, h3, conv_prefix, w["conv_w"], w["conv_b"], w["conv_ln_g"], w["conv_ln_b"],
                               rows=rows, nvalid_last=nvalid if seq == rows else rows,
                               prefix_layer=state_layer)
    mg = merge(lidx, ro.reshape(m, COL), do.reshape(m, COL), co.reshape(m, COL), h, w["w_branch"])
    x = matmul(lidx, mg, w["w_o"], tn=COL, tm=mm_rows, mode="res", res=x)
    xn = rmsnorm(lidx, x, w["norm_ffn"], BF16)
    if seq > rows:
        a, st_g, st_v = ffn_up_gate(lidx, xn, w["ffn_up"], ffn_prefix, w["ffn_dw_w"], w["ffn_dw_b"],
                                    batch=batch, seq=seq, tm=mm_rows, split=8)
    else:
        u = matmul(lidx, xn, w["ffn_up"], tn=COL)
        a, st_g, st_v = ffn_gate(lidx, u.reshape(batch, seq, 2 * D_FF), ffn_prefix, w["ffn_dw_w"], w["ffn_dw_b"],
                                 rows=rows, nvalid_last=nvalid, prefix_layer=state_layer, tn=D_FF)
    x = matmul(lidx, a.reshape(m, D_FF), w["ffn_down"], tn=FFN_TN, tm=mm_rows, mode="res", res=x)
    if k_rows is None:
        k_rows = h3[:, :, CB_DK * COL:(CB_DK + 1) * COL]
        v_rows = h3[:, :, CB_DV * COL:(CB_DV + 1) * COL]
    ffn_new = jnp.concatenate([st_g, st_v], axis=-1)
    return x, (k_rows, v_rows, ret_new, conv_new, ffn_new)


def kernel(x_prompt, x_sample, cache_k, cache_v, page_table, state_ret, state_conv, state_ffn, norm_mix, w_in, lambda_q1, lambda_k1, lambda_q2, lambda_k2, diff_subln, conv_w, conv_b, conv_ln_g, conv_ln_b, w_branch, w_o, norm_ffn, ffn_up, ffn_dw_w, ffn_dw_b, ffn_down, norm_final):
    bp, lp, _ = x_prompt.shape
    bs, ls, _ = x_sample.shape
    depth = w_in.shape[0]
    n_pool = cache_k.shape[1]
    past_len = page_table.shape[1] * PAGE_SIZE
    sr = SAMPLE_ROWS

    vec3 = lambda a: a.reshape(a.shape[0], 1, a.shape[-1])
    w = dict(norm_mix=vec3(norm_mix), w_in=w_in.astype(BF16), conv_w=conv_w, conv_b=vec3(conv_b),
             conv_ln_g=vec3(conv_ln_g), conv_ln_b=vec3(conv_ln_b), w_branch=w_branch.astype(BF16),
             w_o=w_o.astype(BF16), norm_ffn=vec3(norm_ffn), ffn_up=ffn_up.astype(BF16),
             ffn_dw_w=ffn_dw_w, ffn_dw_b=vec3(ffn_dw_b), ffn_down=ffn_down.astype(BF16))
    lam_vecs = tuple(vec3(a) for a in (lambda_q1, lambda_k1, lambda_q2, lambda_k2))
    subln = vec3(diff_subln)
    lam_inits = jnp.asarray([0.8 - 0.6 * math.exp(-0.3 * l) for l in range(depth)], F32)

    pos_p = jnp.arange(lp, dtype=jnp.int32)
    pos_s = past_len + jnp.arange(sr, dtype=jnp.int32)
    rope_p = _rope_tables(pos_p, RET_D) + _rope_tables(pos_p, DIFF_DH)
    rope_s = tuple(jnp.tile(t, (bs, 1)) for t in _rope_tables(pos_s, RET_D) + _rope_tables(pos_s, DIFF_DH))

    cache_kt = jnp.transpose(cache_k, (0, 1, 3, 4, 2))
    cache_v2 = cache_v.reshape(depth, n_pool, PAGE_SIZE * DIFF_HEADS, DIFF_DV)
    conv_pre_s = jnp.pad(state_conv, ((0, 0), (0, 0), (CONV_HALO - (CONV_WIDTH - 1), 0), (0, 0)))
    ffn_pre_s = jnp.pad(state_ffn, ((0, 0), (0, 0), (FFN_HALO - (FFN_CONV_WIDTH - 1), 0), (0, 0)))
    zero_ret = jnp.zeros((bp, RET_HEADS, RET_D, RET_D), F32)
    zero_conv = jnp.zeros((bp, CONV_HALO, CONV_CH), F32)
    zero_ffn = jnp.zeros((bp, FFN_HALO, 2 * D_FF), F32)

    xp = x_prompt.reshape(bp * lp, D_MODEL)
    xs = jnp.pad(x_sample, ((0, 0), (0, sr - ls), (0, 0))).reshape(bs * sr, D_MODEL)

    outs_p, outs_s = [], []
    for layer in range(depth):
        lidx = jnp.full((1,), layer, jnp.int32)
        xp, st_p = _layer(
            lidx, xp, w, rope_p, lam_inits, batch=bp, seq=lp,
            attend=lambda h3: flash_diff_attention(lidx, h3, rope_p[2:], lam_vecs, subln, lam_inits, tq=512),
            ret_state=zero_ret, conv_prefix=zero_conv, ffn_prefix=zero_ffn,
            rows=512, mm_rows=1024, ret_chunk=128, nvalid=128, state_layer=False, rope_in_mm=False)
        xs, st_s = _layer(
            lidx, xs, w, rope_s, lam_inits, batch=bs, seq=sr,
            attend=lambda h3: (decode_diff_attention(lidx, page_table, cache_kt, cache_v2, h3, lam_vecs, subln,
                                                     lam_inits, group=16), None, None),
            ret_state=state_ret, conv_prefix=conv_pre_s, ffn_prefix=ffn_pre_s,
            rows=sr, mm_rows=bs * sr, ret_chunk=sr, nvalid=ls, state_layer=True, rope_in_mm=True)
        outs_p.append(st_p)
        outs_s.append(st_s)

    one = jnp.zeros((1,), jnp.int32)
    y_prompt = rmsnorm(one, xp, norm_final.reshape(1, 1, D_MODEL), F32).reshape(bp, lp, D_MODEL)
    y_sample = rmsnorm(one, xs, norm_final.reshape(1, 1, D_MODEL), F32).reshape(bs, sr, D_MODEL)[:, :ls]

    def stack(outs, i):
        return jnp.stack([o[i] for o in outs])

    kp = stack(outs_p, 0).reshape(depth, bp, lp, 2 * DIFF_HEADS, DIFF_DH)
    vp = stack(outs_p, 1).reshape(depth, bp, lp, DIFF_HEADS, DIFF_DV)
    rp = stack(outs_p, 2)
    cp = stack(outs_p, 3)[:, :, CONV_HALO - (CONV_WIDTH - 1):]
    fp = stack(outs_p, 4)[:, :, FFN_HALO - (FFN_CONV_WIDTH - 1):]
    ks = stack(outs_s, 0)[:, :, :ls].reshape(depth, bs, ls, 2 * DIFF_HEADS, DIFF_DH)
    vs = stack(outs_s, 1)[:, :, :ls].reshape(depth, bs, ls, DIFF_HEADS, DIFF_DV)
    rs = stack(outs_s, 2)
    cs = stack(outs_s, 3)[:, :, CONV_HALO - (CONV_WIDTH - 1):]
    fs = stack(outs_s, 4)[:, :, FFN_HALO - (FFN_CONV_WIDTH - 1):]
    return (y_prompt, y_sample, kp, vp, rp, cp, fp, ks, vs, rs, cs, fs)
```
